```python
import jax, jax.numpy as jnp
from jax import lax
import numpy as np

D_MODEL = 1024
BATCH = 4
SEQ = 8192
DEPTH = 2

BRANCH_WIDTH = D_MODEL // 2
N_BRANCH = 3
LN_EPS = 1e-5
RW_HEAD_DIM = 64
RW_HEADS = BRANCH_WIDTH // RW_HEAD_DIM
RW_DECAY_RANK = 64
RW_ICLR_RANK = 64
RW_GN_EPS = 64e-5
M2_HEAD_DIM = 64
M2_HEADS = BRANCH_WIDTH // M2_HEAD_DIM
M2_GROUPS = 2
M2_STATE = 128
M2_CONV = 4
M2_CHUNK = 128
M2_EPS = 1e-5
GLA_HEADS = 4
GLA_KEY_DIM = BRANCH_WIDTH // 2
GLA_VAL_DIM = BRANCH_WIDTH
GLA_GATE_RANK = 16
GLA_GATE_TAU = 16.0
GLA_CHUNK = 64
GLA_EPS = 1e-5
DEEPNORM_ALPHA = (2.0 * DEPTH) ** 0.25
DEEPNORM_BETA = (8.0 * DEPTH) ** -0.25
RW_SHIFT_WIDTH = 3 * BRANCH_WIDTH + RW_DECAY_RANK + RW_ICLR_RANK
M2_XBC_WIDTH = BRANCH_WIDTH + 2 * M2_GROUPS * M2_STATE
IN_SIZES = (RW_SHIFT_WIDTH, BRANCH_WIDTH,
            M2_XBC_WIDTH, M2_HEADS, BRANCH_WIDTH,
            GLA_KEY_DIM, GLA_KEY_DIM, GLA_VAL_DIM, GLA_GATE_RANK, BRANCH_WIDTH,
            N_BRANCH * D_MODEL)
IN_WIDTH = sum(IN_SIZES)

kernel_name = "hybrid_rwkv7_mamba2_gla_deepnorm_adaln"


def _layer_norm(x, eps):
    xf = x.astype(jnp.float32)
    mu = jnp.mean(xf, axis=-1, keepdims=True)
    var = jnp.mean(jnp.square(xf - mu), axis=-1, keepdims=True)
    return ((xf - mu) * lax.rsqrt(var + eps)).astype(x.dtype)


def _rms_norm(x, eps):
    xf = x.astype(jnp.float32)
    return (xf * lax.rsqrt(jnp.mean(jnp.square(xf), axis=-1, keepdims=True) + eps)).astype(x.dtype)


def _token_shift(x):
    return jnp.pad(x, ((0, 0), (1, 0), (0, 0)))[:, :-1]


def _causal_mask(n):
    return jnp.arange(n)[:, None] >= jnp.arange(n)[None, :]


def _rwkv7_mixer(r, k, v, wd, ad, w0, w_up, a0, a_up, k_k, k_a, r_k, ln_g, ln_b):
    out_dtype = r.dtype
    bsz, seq, _ = r.shape
    f32 = lambda t: t.astype(jnp.float32)
    heads = lambda t: f32(t).reshape(bsz, seq, RW_HEADS, RW_HEAD_DIM)
    w = f32(w0 + jnp.tanh(wd) @ w_up)
    decay = jnp.exp(-jnp.exp(-jax.nn.softplus(-w) - 0.5))
    a = jax.nn.sigmoid(f32(a0 + ad @ a_up))
    kk = heads(k * k_k)
    kk = kk / jnp.maximum(jnp.sqrt(jnp.sum(kk * kk, axis=-1, keepdims=True)), 1e-12)
    k = f32(k) * (1.0 + (a - 1.0) * f32(k_a))
    r, k, v, a, decay = heads(r), heads(k), heads(v), heads(a), heads(decay)

    def step(state, inp):
        r_t, w_t, k_t, v_t, kk_t, a_t = inp
        sa = jnp.einsum('bhij,bhj->bhi', state, -kk_t)
        state = (state * w_t[:, :, None, :]
                 + sa[..., None] * (kk_t * a_t)[:, :, None, :]
                 + v_t[..., None] * k_t[:, :, None, :])
        return state, jnp.einsum('bhij,bhj->bhi', state, r_t)

    state0 = jnp.zeros((bsz, RW_HEADS, RW_HEAD_DIM, RW_HEAD_DIM), jnp.float32)
    sm = lambda t: jnp.moveaxis(t, 1, 0)
    _, y = lax.scan(step, state0, (sm(r), sm(decay), sm(k), sm(v), sm(kk), sm(a)))
    y = jnp.moveaxis(y, 0, 1)
    y = _layer_norm(y, RW_GN_EPS).reshape(bsz, seq, BRANCH_WIDTH) * f32(ln_g) + f32(ln_b)
    bonus = jnp.sum(r * k * f32(r_k), axis=-1, keepdims=True) * v
    return (y + bonus.reshape(bsz, seq, BRANCH_WIDTH)).astype(out_dtype)


def _causal_depthwise_conv(x, w, b):
    y = lax.conv_general_dilated(x, w[:, None, :].astype(x.dtype), (1,), ((w.shape[0] - 1, 0),),
                                 dimension_numbers=('NWC', 'WIO', 'NWC'),
                                 feature_group_count=x.shape[-1])
    return y + b


def _mamba2_mixer(xbc, dt_raw, z, conv_w, conv_b, dt_bias, a_log, d_skip, norm_w):
    out_dtype = z.dtype
    bsz, seq, _ = xbc.shape
    f32 = lambda t: t.astype(jnp.float32)
    nc, hpg = seq // M2_CHUNK, M2_HEADS // M2_GROUPS
    xbc = jax.nn.silu(f32(_causal_depthwise_conv(xbc, conv_w, conv_b)))
    xs, bm, cm = jnp.split(xbc, [BRANCH_WIDTH, BRANCH_WIDTH + M2_GROUPS * M2_STATE], axis=-1)
    xs = xs.reshape(bsz, nc, M2_CHUNK, M2_GROUPS, hpg, M2_HEAD_DIM)
    bm = bm.reshape(bsz, nc, M2_CHUNK, M2_GROUPS, M2_STATE)
    cm = cm.reshape(bsz, nc, M2_CHUNK, M2_GROUPS, M2_STATE)
    dt = jax.nn.softplus(f32(dt_raw) + f32(dt_bias)).reshape(bsz, nc, M2_CHUNK, M2_GROUPS, hpg)
    a_head = -jnp.exp(f32(a_log)).reshape(M2_GROUPS, hpg)
    a_cs = jnp.moveaxis(jnp.cumsum(dt * a_head, axis=2), 2, -1)
    xdt = xs * dt[..., None]
    diff = a_cs[..., :, None] - a_cs[..., None, :]
    decay_in = jnp.exp(jnp.where(_causal_mask(M2_CHUNK), diff, -jnp.inf))
    cb = jnp.einsum('bclgn,bcsgn->bcgls', cm, bm)
    y_diag = jnp.einsum('bcghls,bcsghp->bclghp', cb[:, :, :, None] * decay_in, xdt)
    decay_to_end = jnp.exp(a_cs[..., -1:] - a_cs)
    chunk_states = jnp.einsum('bclgn,bcghl,bclghp->bcghpn', bm, decay_to_end, xdt)
    chunk_decay = jnp.exp(a_cs[..., -1])

    def step(state, inp):
        s_c, d_c = inp
        return state * d_c[..., None, None] + s_c, state

    state0 = jnp.zeros((bsz, M2_GROUPS, hpg, M2_HEAD_DIM, M2_STATE), jnp.float32)
    _, prev = lax.scan(step, state0, (jnp.moveaxis(chunk_states, 1, 0), jnp.moveaxis(chunk_decay, 1, 0)))
    prev = jnp.moveaxis(prev, 0, 1)
    y_off = jnp.einsum('bclgn,bcghpn,bcghl->bclghp', cm, prev, jnp.exp(a_cs))
    y = y_diag + y_off + xs * f32(d_skip).reshape(M2_GROUPS, hpg)[:, :, None]
    y = y.reshape(bsz, seq, BRANCH_WIDTH) * jax.nn.silu(f32(z))
    y = _rms_norm(y.reshape(bsz, seq, M2_GROUPS, BRANCH_WIDTH // M2_GROUPS), M2_EPS)
    return (y.reshape(bsz, seq, BRANCH_WIDTH) * f32(norm_w)).astype(out_dtype)


def _gla_mixer(q, k, v, gk_down, g, gk_up, gk_b, norm_w):
    out_dtype = q.dtype
    bsz, seq, _ = q.shape
    f32 = lambda t: t.astype(jnp.float32)
    nc = seq // GLA_CHUNK
    dk, dv = GLA_KEY_DIM // GLA_HEADS, GLA_VAL_DIM // GLA_HEADS
    log_alpha = jax.nn.log_sigmoid(f32(gk_down @ gk_up + gk_b)) / GLA_GATE_TAU
    chunk = lambda t, d: f32(t).reshape(bsz, nc, GLA_CHUNK, GLA_HEADS, d)
    q = chunk(q, dk) * (dk ** -0.5)
    k = chunk(k, dk)
    v = chunk(v, dv)
    g_cs = jnp.cumsum(chunk(log_alpha, dk), axis=2)
    g_last = g_cs[:, :, -1:]
    q_g = q * jnp.exp(g_cs)
    att = jnp.einsum('bclhd,bcshd->bchls', q_g, k * jnp.exp(-g_cs))
    att = jnp.where(_causal_mask(GLA_CHUNK), att, 0.0)
    o_intra = jnp.einsum('bchls,bcshe->bclhe', att, v)
    chunk_kv = jnp.einsum('bclhd,bclhe->bchde', k * jnp.exp(g_last - g_cs), v)
    chunk_decay = jnp.exp(g_last[:, :, 0])

    def step(state, inp):
        kv_c, d_c = inp
        return state * d_c[..., None] + kv_c, state

    state0 = jnp.zeros((bsz, GLA_HEADS, dk, dv), jnp.float32)
    _, prev = lax.scan(step, state0, (jnp.moveaxis(chunk_kv, 1, 0), jnp.moveaxis(chunk_decay, 1, 0)))
    prev = jnp.moveaxis(prev, 0, 1)
    o = o_intra + jnp.einsum('bclhd,bchde->bclhe', q_g, prev)
    o = _rms_norm(o.reshape(bsz, seq, GLA_HEADS, dv), GLA_EPS) * f32(norm_w)
    return (o.reshape(bsz, seq, GLA_VAL_DIM) * jax.nn.silu(f32(g))).astype(out_dtype)


def _hybrid_layer(x, c, ada_w, ada_b, w_in, rw_mu, rw_w0, rw_w_up, rw_a0, rw_a_up, rw_k_k,
                  rw_k_a, rw_r_k, rw_ln_g, rw_ln_b, m2_conv_w, m2_conv_b, m2_dt_bias, m2_a_log,
                  m2_d_skip, m2_norm_w, gla_gk_up, gla_gk_b, gla_norm_w, w_branch, w_out,
                  post_g, post_b):
    bsz, seq, _ = x.shape
    shift, scale, gate = jnp.split(jax.nn.silu(c) @ ada_w + ada_b, 3, axis=-1)
    h = _layer_norm(x, LN_EPS) * (1.0 + scale[:, None, :]) + shift[:, None, :]
    proj = h @ w_in
    (rw_in, rw_gate, m2_xbc, m2_dt, m2_z, gla_q, gla_k, gla_v, gla_gk, gla_gate,
     merge_logits) = jnp.split(proj, np.cumsum(IN_SIZES)[:-1].tolist(), axis=-1)
    rw_in = rw_in + rw_mu * (_token_shift(rw_in) - rw_in)
    r, k, v, wd, ad = jnp.split(
        rw_in, np.cumsum([BRANCH_WIDTH, BRANCH_WIDTH, BRANCH_WIDTH, RW_DECAY_RANK]).tolist(), axis=-1)
    y_rw = _rwkv7_mixer(r, k, v, wd, ad, rw_w0, rw_w_up, rw_a0, rw_a_up, rw_k_k, rw_k_a, rw_r_k,
                        rw_ln_g, rw_ln_b) * jax.nn.silu(rw_gate)
    y_m2 = _mamba2_mixer(m2_xbc, m2_dt, m2_z, m2_conv_w, m2_conv_b, m2_dt_bias, m2_a_log,
                         m2_d_skip, m2_norm_w)
    y_gla = _gla_mixer(gla_q, gla_k, gla_v, gla_gk, gla_gate, gla_gk_up, gla_gk_b, gla_norm_w)
    gates = jax.nn.sigmoid(merge_logits.astype(jnp.float32)).astype(x.dtype)
    gates = gates.reshape(bsz, seq, N_BRANCH, D_MODEL)
    merged = (gates[:, :, 0] * (y_rw @ w_branch[0])
              + gates[:, :, 1] * (y_m2 @ w_branch[1])
              + gates[:, :, 2] * (y_gla @ w_branch[2]))
    y = merged @ w_out
    res = DEEPNORM_ALPHA * x + (1.0 + gate[:, None, :]) * y
    return _layer_norm(res, LN_EPS) * post_g + post_b


def setup_inputs(seed: int = 0) -> dict:
    key = jax.random.key(seed)
    ks = iter(jax.random.split(key, 32))
    nrm = lambda shape, s: jax.random.normal(next(ks), shape, jnp.float32) * s
    uni = lambda shape, lo, hi: jax.random.uniform(next(ks), shape, jnp.float32, lo, hi)
    L, D, W = DEPTH, D_MODEL, BRANCH_WIDTH
    x = nrm((BATCH, SEQ, D), 1.0)
    c = nrm((BATCH, D), 1.0)
    ada_w = nrm((L, D, 3 * D), 0.5 * D ** -0.5)
    ada_b = nrm((L, 3 * D), 0.01)
    w_in = nrm((L, D, IN_WIDTH), D ** -0.5)
    rw_mu = uni((L, RW_SHIFT_WIDTH), 0.0, 1.0)
    rw_w0 = uni((L, W), -6.0, -1.0)
    rw_w_up = nrm((L, RW_DECAY_RANK, W), RW_DECAY_RANK ** -0.5)
    rw_a0 = nrm((L, W), 0.1)
    rw_a_up = nrm((L, RW_ICLR_RANK, W), RW_ICLR_RANK ** -0.5)
    rw_k_k = 0.85 + nrm((L, W), 0.05)
    rw_k_a = 1.0 + nrm((L, W), 0.05)
    rw_r_k = nrm((L, RW_HEADS, RW_HEAD_DIM), 0.1)
    rw_ln_g = 1.0 + nrm((L, W), 0.02)
    rw_ln_b = nrm((L, W), 0.02)
    m2_conv_w = nrm((L, M2_CONV, M2_XBC_WIDTH), M2_CONV ** -0.5)
    m2_conv_b = nrm((L, M2_XBC_WIDTH), 0.01)
    dt0 = jnp.exp(uni((L, M2_HEADS), float(np.log(1e-3)), float(np.log(1e-1))))
    m2_dt_bias = dt0 + jnp.log(-jnp.expm1(-dt0))
    m2_a_log = jnp.log(uni((L, M2_HEADS), 1.0, 16.0))
    m2_d_skip = 1.0 + nrm((L, M2_HEADS), 0.02)
    m2_norm_w = 1.0 + nrm((L, W), 0.02)
    gla_gk_up = nrm((L, GLA_GATE_RANK, GLA_KEY_DIM), GLA_GATE_RANK ** -0.5)
    gla_gk_b = nrm((L, GLA_KEY_DIM), 0.1)
    gla_norm_w = 1.0 + nrm((L, GLA_VAL_DIM // GLA_HEADS), 0.02)
    w_branch = nrm((L, N_BRANCH, W, D), DEEPNORM_BETA * W ** -0.5)
    w_out = nrm((L, D, D), DEEPNORM_BETA * D ** -0.5)
    post_g = 1.0 + nrm((L, D), 0.02)
    post_b = nrm((L, D), 0.02)
    return {"x": x, "c": c, "ada_w": ada_w, "ada_b": ada_b, "w_in": w_in, "rw_mu": rw_mu,
            "rw_w0": rw_w0, "rw_w_up": rw_w_up, "rw_a0": rw_a0, "rw_a_up": rw_a_up,
            "rw_k_k": rw_k_k, "rw_k_a": rw_k_a, "rw_r_k": rw_r_k, "rw_ln_g": rw_ln_g,
            "rw_ln_b": rw_ln_b, "m2_conv_w": m2_conv_w, "m2_conv_b": m2_conv_b,
            "m2_dt_bias": m2_dt_bias, "m2_a_log": m2_a_log, "m2_d_skip": m2_d_skip,
            "m2_norm_w": m2_norm_w, "gla_gk_up": gla_gk_up, "gla_gk_b": gla_gk_b,
            "gla_norm_w": gla_norm_w, "w_branch": w_branch, "w_out": w_out,
            "post_g": post_g, "post_b": post_b}


def reference(x, c, ada_w, ada_b, w_in, rw_mu, rw_w0, rw_w_up, rw_a0, rw_a_up, rw_k_k, rw_k_a,
              rw_r_k, rw_ln_g, rw_ln_b, m2_conv_w, m2_conv_b, m2_dt_bias, m2_a_log, m2_d_skip,
              m2_norm_w, gla_gk_up, gla_gk_b, gla_norm_w, w_branch, w_out, post_g, post_b):
    for l in range(DEPTH):
        x = _hybrid_layer(x, c, ada_w[l], ada_b[l], w_in[l], rw_mu[l], rw_w0[l], rw_w_up[l],
                          rw_a0[l], rw_a_up[l], rw_k_k[l], rw_k_a[l], rw_r_k[l], rw_ln_g[l],
                          rw_ln_b[l], m2_conv_w[l], m2_conv_b[l], m2_dt_bias[l], m2_a_log[l],
                          m2_d_skip[l], m2_norm_w[l], gla_gk_up[l], gla_gk_b[l], gla_norm_w[l],
                          w_branch[l], w_out[l], post_g[l], post_b[l])
    return x
```

```python
import functools

import jax
import jax.numpy as jnp
from jax import lax
from jax.experimental import pallas as pl
from jax.experimental.pallas import tpu as pltpu

F32 = jnp.float32
BF16 = jnp.bfloat16
HIGHEST = lax.Precision.HIGHEST

D_MODEL = 1024
WIDTH = 512
LN_EPS = 1e-5
RW_HEADS, RW_DIM, RW_RANK = 8, 64, 64
RW_GN_EPS = 64e-5
RW_CHUNK = 64
M2_HEADS, M2_DIM, M2_GROUPS, M2_STATE, M2_CONV = 8, 64, 2, 128, 4
M2_CHUNK = 128
M2_EPS = 1e-5
GLA_HEADS, GLA_DK, GLA_DV, GLA_RANK = 4, 64, 128, 16
GLA_TAU = 16.0
GLA_CHUNK = 64
GLA_EPS = 1e-5
N_BRANCH = 3

LANE = 128
SUBLANE = 8

SEG512 = ("rw_r", "rw_k", "rw_v", "rw_gate", "m2_x", "m2_bc", "m2_z", "gla_qk", "gla_v", "gla_gate")
SEG128 = ("rw_wa", "m2_dt", "gla_gk")
COL512_BASE = N_BRANCH * D_MODEL
COL128_BASE = COL512_BASE + WIDTH * len(SEG512)
PROJ_TILE_N = 512
PROJ_WIDTH = -(-(COL128_BASE + LANE * len(SEG128)) // PROJ_TILE_N) * PROJ_TILE_N


def _blk512(name):
    return (COL512_BASE + WIDTH * SEG512.index(name)) // WIDTH


def _blk128(name):
    return (COL128_BASE + LANE * SEG128.index(name)) // LANE


def _mm(a, b, precision=HIGHEST):
    return lax.dot_general(a, b, (((1,), (0,)), ((), ())), precision=precision,
                           preferred_element_type=F32)


def _mm_nt(a, b, precision=HIGHEST):
    return lax.dot_general(a, b, (((1,), (1,)), ((), ())), precision=precision,
                           preferred_element_type=F32)


def _mm_tn(a, b, precision=HIGHEST):
    return lax.dot_general(a, b, (((0,), (0,)), ((), ())), precision=precision,
                           preferred_element_type=F32)


def _bdot(a, b):
    return jnp.dot(a.astype(BF16), b.astype(BF16), preferred_element_type=F32)


def _sigmoid(x):
    return jax.nn.sigmoid(x)


def _silu(x):
    return x * jax.nn.sigmoid(x)


def _softplus(x):
    return jnp.maximum(x, 0.0) + jnp.log1p(jnp.exp(-jnp.abs(x)))


def _tri(n, strict=False):
    r = lax.broadcasted_iota(jnp.int32, (n, n), 0)
    c = lax.broadcasted_iota(jnp.int32, (n, n), 1)
    return (r > c) if strict else (r >= c)


def _shift_rows(x, carry, j):
    ext = jnp.concatenate([carry, x], axis=0)
    return pltpu.roll(ext, j, 0)[SUBLANE:, :]


def _adaln_kernel(c_ref, w_ref, b_ref, o_ref):
    o_ref[...] = _mm(_silu(c_ref[...]), w_ref[...]) + b_ref[...]


def _adaln(c, ada_w, ada_b):
    depth, d, n = ada_w.shape
    bsz = c.shape[0]
    tn = 768
    return pl.pallas_call(
        _adaln_kernel,
        grid=(depth, n // tn),
        in_specs=[pl.BlockSpec((bsz, d), lambda l, j: (0, 0)),
                  pl.BlockSpec((None, d, tn), lambda l, j: (l, 0, j)),
                  pl.BlockSpec((None, 1, tn), lambda l, j: (l, 0, j))],
        out_specs=pl.BlockSpec((None, bsz, tn), lambda l, j: (l, 0, j)),
        out_shape=jax.ShapeDtypeStruct((depth, bsz, n), F32),
        name="adaln_mod",
    )(c, ada_w, ada_b.reshape(depth, 1, n))


def _inproj_kernel(x_ref, mod_ref, w_ref, o_ref, h_scr):
    @pl.when(pl.program_id(1) == 0)
    def _():
        x = x_ref[...]
        mu = jnp.mean(x, axis=-1, keepdims=True)
        xc = x - mu
        var = jnp.mean(xc * xc, axis=-1, keepdims=True)
        h = xc * lax.rsqrt(var + LN_EPS) * (1.0 + mod_ref[1:2, :]) + mod_ref[0:1, :]
        h_scr[...] = h.astype(BF16)

    o_ref[...] = jnp.dot(h_scr[...], w_ref[...], preferred_element_type=F32)


def _inproj(x2, mod, w_bf16, seq):
    t, d = x2.shape
    n = w_bf16.shape[1]
    tm = min(1024, seq)
    tiles_per_seq = seq // tm
    return pl.pallas_call(
        _inproj_kernel,
        grid=(t // tm, n // PROJ_TILE_N),
        in_specs=[pl.BlockSpec((tm, d), lambda i, j: (i, 0)),
                  pl.BlockSpec((None, 3, d), lambda i, j: (i // tiles_per_seq, 0, 0)),
                  pl.BlockSpec((d, PROJ_TILE_N), lambda i, j: (0, j))],
        out_specs=pl.BlockSpec((tm, PROJ_TILE_N), lambda i, j: (i, j)),
        out_shape=jax.ShapeDtypeStruct((t, n), F32),
        scratch_shapes=[pltpu.VMEM((tm, d), BF16)],
        compiler_params=pltpu.CompilerParams(dimension_semantics=("parallel", "arbitrary")),
        name="inproj",
    )(x2, mod, w_bf16)


def _rwkv_kernel(r_ref, k_ref, v_ref, g_ref, wa_ref,
                 mu_r, mu_k, mu_v, mu_wa, w0, w_up, a0, a_up, k_k, k_a, r_k, ln_g, ln_b, ones_bd,
                 o_ref,
                 car_r, car_k, car_v, car_wa, st, q_r, q_k, q_v, q_kk, q_b, q_lw, y_scr):
    tb = r_ref.shape[0]
    nchunk = tb // RW_CHUNK
    L, N = RW_CHUNK, RW_DIM

    @pl.when(pl.program_id(1) == 0)
    def _():
        car_r[...] = jnp.zeros_like(car_r)
        car_k[...] = jnp.zeros_like(car_k)
        car_v[...] = jnp.zeros_like(car_v)
        car_wa[...] = jnp.zeros_like(car_wa)
        st[...] = jnp.zeros_like(st)

    def lerp(x_ref, car, mu):
        x = x_ref[...]
        sh = _shift_rows(x, car[...], 1)
        car[...] = x[tb - SUBLANE:, :]
        return x + mu[...] * (sh - x)

    r = lerp(r_ref, car_r, mu_r)
    k = lerp(k_ref, car_k, mu_k)
    v = lerp(v_ref, car_v, mu_v)
    wa = lerp(wa_ref, car_wa, mu_wa)
    ones = ones_bd[...]

    w = w0[...] + _mm(jnp.tanh(wa), w_up[...])
    lw = -0.6065306597126334 * _sigmoid(w)
    a = _sigmoid(a0[...] + _mm(wa, a_up[...]))
    kk = k * k_k[...]
    kk = kk / jnp.maximum(jnp.sqrt(_mm(kk * kk, ones)), 1e-12)
    k2 = k * (1.0 + (a - 1.0) * k_a[...])
    bonus = _mm(r * k2 * r_k[...], ones) * v

    q_r[...] = r
    q_k[...] = k2
    q_v[...] = v
    q_kk[...] = kk
    q_b[...] = kk * a
    q_lw[...] = lw

    tri_incl = _tri(L)
    tri_strict = _tri(L, strict=True)
    tri_f = tri_incl.astype(F32)
    eye = (lax.broadcasted_iota(jnp.int32, (L, L), 0)
           == lax.broadcasted_iota(jnp.int32, (L, L), 1)).astype(F32)

    def chunk(c, carry):
        rows = pl.ds(pl.multiple_of(c * L, L), L)
        lwc = q_lw[rows, :]
        g = _mm(tri_f, lwc)
        gl = g[L - 1:L, :]
        eg = jnp.exp(g)
        eng = jnp.exp(-g)
        egl = jnp.exp(gl - g)
        kkc, bc, kc, rc, vc = q_kk[rows, :], q_b[rows, :], q_k[rows, :], q_r[rows, :], q_v[rows, :]
        at = -kkc * jnp.exp(g - lwc)
        bt = bc * eng
        kt = kc * eng
        rt = rc * eg
        bh = bc * egl
        kh = kc * egl
        dec = jnp.exp(gl)
        for h in range(RW_HEADS):
            sl = slice(h * N, (h + 1) * N)
            a_h, b_h, k_h, r_h, v_h = at[:, sl], bt[:, sl], kt[:, sl], rt[:, sl], vc[:, sl]
            a_ab = jnp.where(tri_strict, _mm_nt(a_h, b_h), 0.0)
            a_ak = jnp.where(tri_strict, _mm_nt(a_h, k_h), 0.0)
            a_rb = jnp.where(tri_incl, _mm_nt(r_h, b_h), 0.0)
            a_rk = jnp.where(tri_incl, _mm_nt(r_h, k_h), 0.0)
            p = a_ab
            t = eye + a_ab
            for _ in range(5):
                p = _mm(p, p)
                t = t + _mm(t, p)
            s0 = st[h]
            u = _mm(t, _mm_nt(a_h, s0) + _mm(a_ak, v_h))
            y = _mm_nt(r_h, s0) + _mm(a_rb, u) + _mm(a_rk, v_h)
            st[h] = s0 * dec[:, sl] + _mm_tn(u, bh[:, sl]) + _mm_tn(v_h, kh[:, sl])
            y_scr[rows, sl] = y
        return carry

    lax.fori_loop(0, nchunk, chunk, 0)

    y = y_scr[...]
    mean = _mm(y, ones) * (1.0 / N)
    yc = y - mean
    var = _mm(yc * yc, ones) * (1.0 / N)
    yn = yc * lax.rsqrt(var + RW_GN_EPS) * ln_g[...] + ln_b[...]
    o_ref[...] = (yn + bonus) * _silu(g_ref[...])


def _rwkv(proj, p, bsz, seq):
    tb = min(512, seq)
    spt = seq // tb
    row = lambda b, s: b * spt + s
    seg = lambda name: pl.BlockSpec((tb, WIDTH), lambda b, s, j=_blk512(name): (row(b, s), j))
    vec = lambda n: pl.BlockSpec((1, n), lambda b, s: (0, 0))
    mat = lambda m, n: pl.BlockSpec((m, n), lambda b, s: (0, 0))
    tile = lambda: pltpu.VMEM((tb, WIDTH), F32)
    return pl.pallas_call(
        _rwkv_kernel,
        grid=(bsz, spt),
        in_specs=[seg("rw_r"), seg("rw_k"), seg("rw_v"), seg("rw_gate"),
                  pl.BlockSpec((tb, LANE), lambda b, s, j=_blk128("rw_wa"): (row(b, s), j)),
                  vec(WIDTH), vec(WIDTH), vec(WIDTH), vec(LANE), vec(WIDTH), mat(LANE, WIDTH),
                  vec(WIDTH), mat(LANE, WIDTH), vec(WIDTH), vec(WIDTH), vec(WIDTH), vec(WIDTH),
                  vec(WIDTH), mat(WIDTH, WIDTH)],
        out_specs=pl.BlockSpec((tb, WIDTH), lambda b, s: (row(b, s), 0)),
        out_shape=jax.ShapeDtypeStruct((bsz * seq, WIDTH), F32),
        scratch_shapes=[pltpu.VMEM((SUBLANE, WIDTH), F32), pltpu.VMEM((SUBLANE, WIDTH), F32),
                        pltpu.VMEM((SUBLANE, WIDTH), F32), pltpu.VMEM((SUBLANE, LANE), F32),
                        pltpu.VMEM((RW_HEADS, RW_DIM, RW_DIM), F32),
                        tile(), tile(), tile(), tile(), tile(), tile(), tile()],
        compiler_params=pltpu.CompilerParams(dimension_semantics=("arbitrary", "arbitrary")),
        name="rwkv7_mixer",
    )(proj, proj, proj, proj, proj,
      p["mu_r"], p["mu_k"], p["mu_v"], p["mu_wa"], p["w0"], p["w_up"], p["a0"], p["a_up"],
      p["k_k"], p["k_a"], p["r_k"], p["ln_g"], p["ln_b"], p["ones_bd"])


def _mamba_kernel(x_ref, bc_ref, z_ref, dt_ref,
                  cw_x, cw_bc, cb_x, cb_bc, dt_bias, a_log, d_skip, norm_w, expand,
                  o_ref,
                  car_x, car_bc, st, q_x, q_xdt, q_bc, q_da, y_scr):
    tb = x_ref.shape[0]
    L = M2_CHUNK
    nchunk = tb // L
    gw = WIDTH // M2_GROUPS
    hpg = M2_HEADS // M2_GROUPS

    @pl.when(pl.program_id(1) == 0)
    def _():
        car_x[...] = jnp.zeros_like(car_x)
        car_bc[...] = jnp.zeros_like(car_bc)
        st[...] = jnp.zeros_like(st)

    def conv(x_ref, car, w, b):
        x = x_ref[...]
        c = car[...]
        y = x * w[M2_CONV - 1:M2_CONV, :] + b[...]
        for j in range(1, M2_CONV):
            y = y + _shift_rows(x, c, j) * w[M2_CONV - 1 - j:M2_CONV - j, :]
        car[...] = x[tb - SUBLANE:, :]
        return _silu(y)

    xs = conv(x_ref, car_x, cw_x, cb_x)
    bcs = conv(bc_ref, car_bc, cw_bc, cb_bc)
    ex = expand[...]
    dt = _softplus(dt_ref[...] + dt_bias[...])
    q_da[...] = dt * (-jnp.exp(a_log[...]))
    q_x[...] = xs
    q_xdt[...] = xs * _mm(dt, ex)
    q_bc[...] = bcs

    tri_incl = _tri(L)
    tri_f = tri_incl.astype(F32)

    def chunk(c, carry):
        rows = pl.ds(pl.multiple_of(c * L, L), L)
        a_cs = _mm(tri_f, q_da[rows, :])
        a_cs_t = a_cs.T
        a_x = _mm(a_cs, ex)
        a_last = a_x[L - 1:L, :]
        xdt = q_xdt[rows, :]
        xdt_end = xdt * jnp.exp(a_last - a_x)
        e_ax = jnp.exp(a_x)
        e_last = jnp.exp(a_last)
        bcc = q_bc[rows, :]
        for g in range(M2_GROUPS):
            bm = bcc[:, g * M2_STATE:(g + 1) * M2_STATE]
            cm = bcc[:, gw + g * M2_STATE:gw + (g + 1) * M2_STATE]
            cb = _mm_nt(cm, bm)
            gs = slice(g * gw, (g + 1) * gw)
            s0 = st[g]
            y_off = _mm(cm, s0) * e_ax[:, gs]
            st[g] = s0 * e_last[:, gs] + _mm_tn(bm, xdt_end[:, gs])
            for hh in range(hpg):
                h = g * hpg + hh
                diff = a_cs[:, h:h + 1] - a_cs_t[h:h + 1, :]
                dec = jnp.exp(jnp.where(tri_incl, diff, -jnp.inf))
                hs = slice(h * M2_DIM, (h + 1) * M2_DIM)
                y_scr[rows, hs] = (_mm(cb * dec, xdt[:, hs])
                                   + y_off[:, hh * M2_DIM:(hh + 1) * M2_DIM])
        return carry

    lax.fori_loop(0, nchunk, chunk, 0)

    y = (y_scr[...] + q_x[...] * d_skip[...]) * _silu(z_ref[...])
    outs = []
    for g in range(M2_GROUPS):
        yg = y[:, g * gw:(g + 1) * gw]
        ms = jnp.mean(yg * yg, axis=-1, keepdims=True)
        outs.append(yg * lax.rsqrt(ms + M2_EPS))
    o_ref[...] = jnp.concatenate(outs, axis=-1) * norm_w[...]


def _mamba(proj, p, bsz, seq):
    tb = min(512, seq)
    spt = seq // tb
    row = lambda b, s: b * spt + s
    seg = lambda name: pl.BlockSpec((tb, WIDTH), lambda b, s, j=_blk512(name): (row(b, s), j))
    vec = lambda n: pl.BlockSpec((1, n), lambda b, s: (0, 0))
    mat = lambda m, n: pl.BlockSpec((m, n), lambda b, s: (0, 0))
    tile = lambda n=WIDTH: pltpu.VMEM((tb, n), F32)
    return pl.pallas_call(
        _mamba_kernel,
        grid=(bsz, spt),
        in_specs=[seg("m2_x"), seg("m2_bc"), seg("m2_z"),
                  pl.BlockSpec((tb, LANE), lambda b, s, j=_blk128("m2_dt"): (row(b, s), j)),
                  mat(M2_CONV, WIDTH), mat(M2_CONV, WIDTH), vec(WIDTH), vec(WIDTH),
                  vec(LANE), vec(LANE), vec(WIDTH), vec(WIDTH), mat(LANE, WIDTH)],
        out_specs=pl.BlockSpec((tb, WIDTH), lambda b, s: (row(b, s), 0)),
        out_shape=jax.ShapeDtypeStruct((bsz * seq, WIDTH), F32),
        scratch_shapes=[pltpu.VMEM((SUBLANE, WIDTH), F32), pltpu.VMEM((SUBLANE, WIDTH), F32),
                        pltpu.VMEM((M2_GROUPS, M2_STATE, WIDTH // M2_GROUPS), F32),
                        tile(), tile(), tile(), tile(LANE), tile()],
        compiler_params=pltpu.CompilerParams(dimension_semantics=("arbitrary", "arbitrary")),
        name="mamba2_mixer",
    )(proj, proj, proj, proj,
      p["cw_x"], p["cw_bc"], p["cb_x"], p["cb_bc"], p["dt_bias"], p["a_log"], p["d_skip"],
      p["m2_norm_w"], p["expand"])


def _gla_kernel(qk_ref, v_ref, g_ref, gk_ref, gk_up, gk_b, norm_w,
                o_ref, st, q_la, o_scr):
    tb = qk_ref.shape[0]
    L = GLA_CHUNK
    nchunk = tb // L
    kd = GLA_HEADS * GLA_DK

    @pl.when(pl.program_id(1) == 0)
    def _():
        st[...] = jnp.zeros_like(st)

    q_la[...] = -_softplus(-(_mm(gk_ref[...], gk_up[...]) + gk_b[...])) * (1.0 / GLA_TAU)

    tri_incl = _tri(L)
    tri_f = tri_incl.astype(F32)

    def chunk(c, carry):
        rows = pl.ds(pl.multiple_of(c * L, L), L)
        g = _mm(tri_f, q_la[rows, :])
        gl = g[L - 1:L, :]
        qk = qk_ref[rows, :]
        q = qk[:, :kd] * (GLA_DK ** -0.5)
        k = qk[:, kd:]
        qg = q * jnp.exp(g)
        kg = k * jnp.exp(-g)
        kh = k * jnp.exp(gl - g)
        dec = jnp.exp(gl)
        vc = v_ref[rows, :]
        for h in range(GLA_HEADS):
            ks = slice(h * GLA_DK, (h + 1) * GLA_DK)
            vs = slice(h * GLA_DV, (h + 1) * GLA_DV)
            att = jnp.where(tri_incl, _mm_nt(qg[:, ks], kg[:, ks]), 0.0)
            s0 = st[h]
            o = _mm(att, vc[:, vs]) + _mm_nt(qg[:, ks], s0)
            st[h] = s0 * dec[:, ks] + _mm_tn(vc[:, vs], kh[:, ks])
            o_scr[rows, vs] = o
        return carry

    lax.fori_loop(0, nchunk, chunk, 0)

    o = o_scr[...]
    outs = []
    for h in range(GLA_HEADS):
        oh = o[:, h * GLA_DV:(h + 1) * GLA_DV]
        ms = jnp.mean(oh * oh, axis=-1, keepdims=True)
        outs.append(oh * lax.rsqrt(ms + GLA_EPS))
    o_ref[...] = jnp.concatenate(outs, axis=-1) * norm_w[...] * _silu(g_ref[...])


def _gla(proj, p, bsz, seq):
    tb = min(512, seq)
    spt = seq // tb
    row = lambda b, s: b * spt + s
    seg = lambda name: pl.BlockSpec((tb, WIDTH), lambda b, s, j=_blk512(name): (row(b, s), j))
    return pl.pallas_call(
        _gla_kernel,
        grid=(bsz, spt),
        in_specs=[seg("gla_qk"), seg("gla_v"), seg("gla_gate"),
                  pl.BlockSpec((tb, LANE), lambda b, s, j=_blk128("gla_gk"): (row(b, s), j)),
                  pl.BlockSpec((LANE, GLA_HEADS * GLA_DK), lambda b, s: (0, 0)),
                  pl.BlockSpec((1, GLA_HEADS * GLA_DK), lambda b, s: (0, 0)),
                  pl.BlockSpec((1, WIDTH), lambda b, s: (0, 0))],
        out_specs=pl.BlockSpec((tb, WIDTH), lambda b, s: (row(b, s), 0)),
        out_shape=jax.ShapeDtypeStruct((bsz * seq, WIDTH), F32),
        scratch_shapes=[pltpu.VMEM((GLA_HEADS, GLA_DV, GLA_DK), F32),
                        pltpu.VMEM((tb, GLA_HEADS * GLA_DK), F32),
                        pltpu.VMEM((tb, WIDTH), F32)],
        compiler_params=pltpu.CompilerParams(dimension_semantics=("arbitrary", "arbitrary")),
        name="gla_mixer",
    )(proj, proj, proj, proj, p["gk_up"], p["gk_b"], p["gla_norm_w"])


def _merge_kernel(alpha, y_rw, y_m2, y_gla, lg0, lg1, lg2, x_ref, mod_ref, wb, w_out, pg, pb,
                  o_ref):
    merged = (_sigmoid(lg0[...]) * _bdot(y_rw[...], wb[0])
              + _sigmoid(lg1[...]) * _bdot(y_m2[...], wb[1])
              + _sigmoid(lg2[...]) * _bdot(y_gla[...], wb[2]))
    y = _bdot(merged, w_out[...])
    res = alpha * x_ref[...] + (1.0 + mod_ref[2:3, :]) * y
    mu = jnp.mean(res, axis=-1, keepdims=True)
    rc = res - mu
    var = jnp.mean(rc * rc, axis=-1, keepdims=True)
    o_ref[...] = rc * lax.rsqrt(var + LN_EPS) * pg[...] + pb[...]


def _merge(y_rw, y_m2, y_gla, proj, x2, mod, p, seq, alpha):
    t, d = x2.shape
    tm = min(512, seq)
    tiles_per_seq = seq // tm
    ytile = lambda: pl.BlockSpec((tm, WIDTH), lambda i: (i, 0))
    logit = lambda j: pl.BlockSpec((tm, d), lambda i, j=j: (i, j))
    return pl.pallas_call(
        functools.partial(_merge_kernel, alpha),
        grid=(t // tm,),
        in_specs=[ytile(), ytile(), ytile(), logit(0), logit(1), logit(2),
                  pl.BlockSpec((tm, d), lambda i: (i, 0)),
                  pl.BlockSpec((None, 3, d), lambda i: (i // tiles_per_seq, 0, 0)),
                  pl.BlockSpec((N_BRANCH, WIDTH, d), lambda i: (0, 0, 0)),
                  pl.BlockSpec((d, d), lambda i: (0, 0)),
                  pl.BlockSpec((1, d), lambda i: (0, 0)),
                  pl.BlockSpec((1, d), lambda i: (0, 0))],
        out_specs=pl.BlockSpec((tm, d), lambda i: (i, 0)),
        out_shape=jax.ShapeDtypeStruct((t, d), F32),
        compiler_params=pltpu.CompilerParams(dimension_semantics=("parallel",)),
        name="merge_out",
    )(y_rw, y_m2, y_gla, proj, proj, proj, x2, mod, p["w_branch"], p["w_out"], p["post_g"],
      p["post_b"])


def _pad_cols(a, n):
    return jnp.pad(a, ((0, 0), (0, n - a.shape[1])))


def _pad_rows_at(a, start, n):
    return jnp.pad(a, ((start, n - start - a.shape[0]), (0, 0)))


def _layer_params(l, w_in, rw_mu, rw_w0, rw_w_up, rw_a0, rw_a_up, rw_k_k, rw_k_a, rw_r_k,
                  rw_ln_g, rw_ln_b, m2_conv_w, m2_conv_b, m2_dt_bias, m2_a_log, m2_d_skip,
                  m2_norm_w, gla_gk_up, gla_gk_b, gla_norm_w, w_branch, w_out, post_g, post_b):
    W = WIDTH
    rank2 = 2 * RW_RANK
    sizes = (3 * W + rank2, W, W + 2 * M2_GROUPS * M2_STATE, M2_HEADS, W,
             GLA_HEADS * GLA_DK, GLA_HEADS * GLA_DK, GLA_HEADS * GLA_DV, GLA_RANK, W,
             N_BRANCH * D_MODEL)
    offs = [0]
    for s in sizes:
        offs.append(offs[-1] + s)
    (o_rw, o_rwg, o_xbc, o_dt, o_z, o_q, o_k, o_v, o_gk, o_gg, o_merge, _) = offs
    wl = w_in[l]
    cols = lambda a, n: wl[:, a:a + n]
    segs = {
        "rw_r": cols(o_rw, W), "rw_k": cols(o_rw + W, W), "rw_v": cols(o_rw + 2 * W, W),
        "rw_gate": cols(o_rwg, W), "m2_x": cols(o_xbc, W), "m2_bc": cols(o_xbc + W, W),
        "m2_z": cols(o_z, W), "gla_qk": cols(o_q, W), "gla_v": cols(o_v, W),
        "gla_gate": cols(o_gg, W),
        "rw_wa": cols(o_rw + 3 * W, rank2), "m2_dt": _pad_cols(cols(o_dt, M2_HEADS), LANE),
        "gla_gk": _pad_cols(cols(o_gk, GLA_RANK), LANE),
    }
    w_cat = jnp.concatenate([cols(o_merge, N_BRANCH * D_MODEL)] + [segs[n] for n in SEG512]
                            + [segs[n] for n in SEG128], axis=1)
    w_cat = _pad_cols(w_cat, PROJ_WIDTH).astype(BF16)

    row = lambda a: a.reshape(1, -1)
    mu = rw_mu[l]
    head_of_ch = jnp.arange(W) // RW_DIM
    ones_bd = (head_of_ch[:, None] == head_of_ch[None, :]).astype(F32)
    expand = (jnp.arange(LANE)[:, None] == (jnp.arange(W) // M2_DIM)[None, :]).astype(F32)
    cw = m2_conv_w[l]
    cb = m2_conv_b[l]
    return {
        "w_in": w_cat,
        "mu_r": row(mu[:W]), "mu_k": row(mu[W:2 * W]), "mu_v": row(mu[2 * W:3 * W]),
        "mu_wa": row(mu[3 * W:]),
        "w0": row(rw_w0[l]), "w_up": _pad_rows_at(rw_w_up[l], 0, LANE),
        "a0": row(rw_a0[l]), "a_up": _pad_rows_at(rw_a_up[l], RW_RANK, LANE),
        "k_k": row(rw_k_k[l]), "k_a": row(rw_k_a[l]), "r_k": row(rw_r_k[l]),
        "ln_g": row(rw_ln_g[l]), "ln_b": row(rw_ln_b[l]), "ones_bd": ones_bd,
        "cw_x": cw[:, :W], "cw_bc": cw[:, W:], "cb_x": row(cb[:W]), "cb_bc": row(cb[W:]),
        "dt_bias": _pad_cols(row(m2_dt_bias[l]), LANE), "a_log": _pad_cols(row(m2_a_log[l]), LANE),
        "d_skip": row(jnp.repeat(m2_d_skip[l], M2_DIM)), "m2_norm_w": row(m2_norm_w[l]),
        "expand": expand,
        "gk_up": _pad_rows_at(gla_gk_up[l], 0, LANE), "gk_b": row(gla_gk_b[l]),
        "gla_norm_w": row(jnp.tile(gla_norm_w[l], GLA_HEADS)),
        "w_branch": w_branch[l].astype(BF16), "w_out": w_out[l].astype(BF16),
        "post_g": row(post_g[l]), "post_b": row(post_b[l]),
    }


def kernel(x, c, ada_w, ada_b, w_in, rw_mu, rw_w0, rw_w_up, rw_a0, rw_a_up, rw_k_k, rw_k_a, rw_r_k, rw_ln_g, rw_ln_b, m2_conv_w, m2_conv_b, m2_dt_bias, m2_a_log, m2_d_skip, m2_norm_w, gla_gk_up, gla_gk_b, gla_norm_w, w_branch, w_out, post_g, post_b):
    bsz, seq, d = x.shape
    depth = ada_w.shape[0]
    alpha = (2.0 * depth) ** 0.25
    mod_all = _adaln(c, ada_w, ada_b).reshape(depth, bsz, 3, d)
    x2 = x.reshape(bsz * seq, d)
    for l in range(depth):
        p = _layer_params(l, w_in, rw_mu, rw_w0, rw_w_up, rw_a0, rw_a_up, rw_k_k, rw_k_a,
                          rw_r_k, rw_ln_g, rw_ln_b, m2_conv_w, m2_conv_b, m2_dt_bias, m2_a_log,
                          m2_d_skip, m2_norm_w, gla_gk_up, gla_gk_b, gla_norm_w, w_branch, w_out,
                          post_g, post_b)
        mod = mod_all[l]
        proj = _inproj(x2, mod, p["w_in"], seq)
        y_rw = _rwkv(proj, p, bsz, seq)
        y_m2 = _mamba(proj, p, bsz, seq)
        y_gla = _gla(proj, p, bsz, seq)
        x2 = _merge(y_rw, y_m2, y_gla, proj, x2, mod, p, seq, alpha)
    return x2.reshape(bsz, seq, d)
```

```python
import functools

import jax
import jax.numpy as jnp
from jax import lax
from jax.experimental import pallas as pl
from jax.experimental.pallas import tpu as pltpu

F32 = jnp.float32
BF16 = jnp.bfloat16
HIGHEST = lax.Precision.HIGHEST

D_MODEL = 1024
WIDTH = 512
LN_EPS = 1e-5
RW_HEADS, RW_DIM, RW_RANK = 8, 64, 64
RW_GN_EPS = 64e-5
RW_CHUNK = 64
RW_SUBCHUNKS = 2
M2_HEADS, M2_DIM, M2_GROUPS, M2_STATE, M2_CONV = 8, 64, 2, 128, 4
M2_CHUNK = 128
M2_EPS = 1e-5
GLA_HEADS, GLA_DK, GLA_DV, GLA_RANK = 4, 64, 128, 16
GLA_TAU = 16.0
GLA_CHUNK = 64
GLA_SUBCHUNKS = 2
GLA_EPS = 1e-5
N_BRANCH = 3

LANE = 128
SUBLANE = 8

SEG512 = ("rw_r", "rw_k", "rw_v", "rw_gate", "m2_x", "m2_bc", "m2_z", "gla_qk", "gla_v", "gla_gate")
SEG128 = ("rw_wa", "m2_dt", "gla_gk")
COL512_BASE = N_BRANCH * D_MODEL
COL128_BASE = COL512_BASE + WIDTH * len(SEG512)
PROJ_WIDTH = -(-(COL128_BASE + LANE * len(SEG128)) // WIDTH) * WIDTH
PROJ_COL_TILES = 4
PROJ_TILE_N = PROJ_WIDTH // PROJ_COL_TILES
assert PROJ_TILE_N * PROJ_COL_TILES == PROJ_WIDTH and PROJ_TILE_N % LANE == 0


def _blk512(name):
    return (COL512_BASE + WIDTH * SEG512.index(name)) // WIDTH


def _blk128(name):
    return (COL128_BASE + LANE * SEG128.index(name)) // LANE


def _mm(a, b, precision=HIGHEST):
    return lax.dot_general(a, b, (((1,), (0,)), ((), ())), precision=precision,
                           preferred_element_type=F32)


def _mm_nt(a, b, precision=HIGHEST):
    return lax.dot_general(a, b, (((1,), (1,)), ((), ())), precision=precision,
                           preferred_element_type=F32)


def _mm_tn(a, b, precision=HIGHEST):
    return lax.dot_general(a, b, (((0,), (0,)), ((), ())), precision=precision,
                           preferred_element_type=F32)


def _bdot(a, b):
    return jnp.dot(a.astype(BF16), b.astype(BF16), preferred_element_type=F32)


P_A = 1
P_INV = 1
P_B = 1
P_S = 3
P_M2 = 1
P_GLA = 1

NN = ((1,), (0,))
NT = ((1,), (1,))
TN = ((0,), (0,))


def _dg(a, b, dims):
    return lax.dot_general(a, b, (dims, ((), ())), preferred_element_type=F32)


def _split2(x):
    hi = x.astype(BF16)
    return hi, (x - hi.astype(F32)).astype(BF16)


def _dot(a, b, dims=NN, passes=1):
    if passes == 6:
        return lax.dot_general(a, b, (dims, ((), ())), precision=HIGHEST,
                               preferred_element_type=F32)
    if passes == 1:
        return _dg(a.astype(BF16), b.astype(BF16), dims)
    a_hi, a_lo = _split2(a)
    b_hi, b_lo = _split2(b)
    return _dg(a_hi, b_hi, dims) + _dg(a_lo, b_hi, dims) + _dg(a_hi, b_lo, dims)


def _dot_exact_rhs(a, b_bf16, pieces=2):
    out = None
    for _ in range(pieces):
        piece = a.astype(BF16)
        a = a - piece.astype(F32)
        term = _dg(piece, b_bf16, NN)
        out = term if out is None else out + term
    return out


def _chunk_cumsum(x, chunk):
    pos = lax.broadcasted_iota(jnp.int32, x.shape, 0) & (chunk - 1)
    step = 1
    while step < chunk:
        x = x + jnp.where(pos >= step, pltpu.roll(x, step, 0), 0.0)
        step *= 2
    return x


def _sigmoid(x):
    return jax.nn.sigmoid(x)


def _silu(x):
    return x * jax.nn.sigmoid(x)


def _softplus(x):
    return jnp.maximum(x, 0.0) + jnp.log1p(jnp.exp(-jnp.abs(x)))


def _tri(n, strict=False):
    r = lax.broadcasted_iota(jnp.int32, (n, n), 0)
    c = lax.broadcasted_iota(jnp.int32, (n, n), 1)
    return (r > c) if strict else (r >= c)


def _shift_rows(x, carry, j):
    ext = jnp.concatenate([carry, x], axis=0)
    return pltpu.roll(ext, j, 0)[SUBLANE:, :]


def _adaln_kernel(c_ref, w_ref, b_ref, o_ref):
    o_ref[...] = _mm(_silu(c_ref[...]), w_ref[...]) + b_ref[...]


def _adaln(c, ada_w, ada_b):
    depth, d, n = ada_w.shape
    bsz = c.shape[0]
    tn = 768
    return pl.pallas_call(
        _adaln_kernel,
        grid=(depth, n // tn),
        in_specs=[pl.BlockSpec((bsz, d), lambda l, j: (0, 0)),
                  pl.BlockSpec((None, d, tn), lambda l, j: (l, 0, j)),
                  pl.BlockSpec((None, 1, tn), lambda l, j: (l, 0, j))],
        out_specs=pl.BlockSpec((None, bsz, tn), lambda l, j: (l, 0, j)),
        out_shape=jax.ShapeDtypeStruct((depth, bsz, n), F32),
        name="adaln_mod",
    )(c, ada_w, ada_b.reshape(depth, 1, n))


def _inproj_kernel(x_ref, mod_ref, w_ref, o_ref, h_scr):
    @pl.when(pl.program_id(1) == 0)
    def _():
        x = x_ref[...]
        mu = jnp.mean(x, axis=-1, keepdims=True)
        xc = x - mu
        var = jnp.mean(xc * xc, axis=-1, keepdims=True)
        h = xc * lax.rsqrt(var + LN_EPS) * (1.0 + mod_ref[1:2, :]) + mod_ref[0:1, :]
        h_scr[...] = h.astype(BF16)

    o_ref[...] = jnp.dot(h_scr[...], w_ref[...], preferred_element_type=F32)


def _inproj(x2, mod, w_bf16, seq):
    t, d = x2.shape
    n = w_bf16.shape[1]
    tm = min(1024, seq)
    tiles_per_seq = seq // tm
    return pl.pallas_call(
        _inproj_kernel,
        grid=(t // tm, n // PROJ_TILE_N),
        in_specs=[pl.BlockSpec((tm, d), lambda i, j: (i, 0)),
                  pl.BlockSpec((None, 3, d), lambda i, j: (i // tiles_per_seq, 0, 0)),
                  pl.BlockSpec((d, PROJ_TILE_N), lambda i, j: (0, j))],
        out_specs=pl.BlockSpec((tm, PROJ_TILE_N), lambda i, j: (i, j)),
        out_shape=jax.ShapeDtypeStruct((t, n), F32),
        scratch_shapes=[pltpu.VMEM((tm, d), BF16)],
        compiler_params=pltpu.CompilerParams(dimension_semantics=("parallel", "arbitrary")),
        name="inproj",
    )(x2, mod, w_bf16)


def _rwkv_kernel(r_ref, k_ref, v_ref, g_ref, wa_ref,
                 mu_r, mu_k, mu_v, mu_wa, w0, w_up, a0, a_up, k_k, k_a, r_k, ln_g, ln_b, ones_bd,
                 o_ref,
                 car_r, car_k, car_v, car_wa, st, q_at, q_bt, q_kt, q_rt, q_b, q_k, q_v, q_g,
                 y_scr):
    tb = r_ref.shape[0]
    nchunk = tb // RW_CHUNK
    L, N = RW_CHUNK, RW_DIM

    @pl.when(pl.program_id(1) == 0)
    def _():
        car_r[...] = jnp.zeros_like(car_r)
        car_k[...] = jnp.zeros_like(car_k)
        car_v[...] = jnp.zeros_like(car_v)
        car_wa[...] = jnp.zeros_like(car_wa)
        st[...] = jnp.zeros_like(st)

    def lerp(x_ref, car, mu):
        x = x_ref[...]
        sh = _shift_rows(x, car[...], 1)
        car[...] = x[tb - SUBLANE:, :]
        return x + mu[...] * (sh - x)

    r = lerp(r_ref, car_r, mu_r)
    k = lerp(k_ref, car_k, mu_k)
    v = lerp(v_ref, car_v, mu_v)
    wa = lerp(wa_ref, car_wa, mu_wa)
    ones = ones_bd[...]
    head_sum = lambda t: _dot_exact_rhs(t, ones)

    w = w0[...] + _dot(jnp.tanh(wa), w_up[...], NN, 3)
    lw = -0.6065306597126334 * _sigmoid(w)
    a = _sigmoid(a0[...] + _dot(wa, a_up[...], NN, 3))
    kk = k * k_k[...]
    kk = kk / jnp.maximum(jnp.sqrt(head_sum(kk * kk)), 1e-12)
    k2 = k * (1.0 + (a - 1.0) * k_a[...])
    bonus = head_sum(r * k2 * r_k[...]) * v
    b = kk * a

    g = _chunk_cumsum(lw, L)
    eng = jnp.exp(-g)
    q_at[...] = -kk * jnp.exp(g - lw)
    q_bt[...] = b * eng
    q_kt[...] = k2 * eng
    q_rt[...] = r * jnp.exp(g)
    q_b[...] = b
    q_k[...] = k2
    q_v[...] = v
    q_g[...] = g

    P2 = 2 * L
    npair = RW_HEADS // 2
    lane_lo = lax.broadcasted_iota(jnp.int32, (L, P2), 1) < N

    def hat(x):
        return jnp.concatenate([jnp.where(lane_lo, x, 0.0), jnp.where(lane_lo, 0.0, x)], axis=0)

    ri = lax.broadcasted_iota(jnp.int32, (2 * P2, 2 * P2), 0)
    ci = lax.broadcasted_iota(jnp.int32, (2 * P2, 2 * P2), 1)
    same_head = ((ri & (P2 - 1)) >= L) == ((ci & (P2 - 1)) >= L)
    tpos = ri & (L - 1)
    spos = ci & (L - 1)
    mask4 = same_head & (jnp.where(ri < P2, tpos, tpos + 1) > spos)
    eye = (lax.broadcasted_iota(jnp.int32, (P2, P2), 0)
           == lax.broadcasted_iota(jnp.int32, (P2, P2), 1)).astype(F32)
    nsub = RW_SUBCHUNKS

    def chunk(it, carry):
        probs = []
        for i in range(nsub):
            rows = pl.ds(pl.multiple_of((it * nsub + i) * L, L), L)
            gc = q_g[rows, :]
            gl = gc[L - 1:L, :]
            egl = jnp.exp(gl - gc)
            dec = jnp.exp(gl)
            bh = q_b[rows, :] * egl
            kh = q_k[rows, :] * egl
            for j in range(npair):
                ls = slice(j * P2, (j + 1) * P2)
                probs.append(dict(
                    rows=rows, ls=ls, j=j, dec=dec[:, ls],
                    a=hat(q_at[rows, ls]), r=hat(q_rt[rows, ls]), b=hat(q_bt[rows, ls]),
                    k=hat(q_kt[rows, ls]), v=hat(q_v[rows, ls]), bh=hat(bh[:, ls]),
                    kh=hat(kh[:, ls])))
        for q in probs:
            m = _dot(jnp.concatenate([q["a"], q["r"]], axis=0),
                     jnp.concatenate([q["b"], q["k"]], axis=0), NT, P_A)
            m = jnp.where(mask4, m, 0.0)
            q["a_ab"], q["a_ak"] = m[:P2, :P2], m[:P2, P2:]
            q["a_rb"], q["a_rk"] = m[P2:, :P2], m[P2:, P2:]
            q["p"] = q["a_ab"]
            q["t"] = eye + q["a_ab"]
        for _ in range(5):
            for q in probs:
                q["p"] = _dot(q["p"], q["p"], NN, P_INV)
            for q in probs:
                q["t"] = q["t"] + _dot(q["t"], q["p"], NN, P_INV)
        for q in probs:
            q["av"] = _dot(q["a_ak"], q["v"], NN, P_B)
        for q in probs:
            q["tu"] = _dot(q["t"], jnp.concatenate([q["a"], q["av"]], axis=1), NN, P_B)
        for q in probs:
            ru = _dot(q["a_rb"], q["tu"], NN, P_B)
            q["rm"] = q["r"] + ru[:, :P2]
            q["y0"] = ru[:, P2:] + _dot(q["a_rk"], q["v"], NN, P_B)
        for q in probs:
            q["mt"] = eye * q["dec"] + _dot(q["tu"][:, :P2], q["bh"], TN, P_B)
            q["cc"] = _dot(jnp.concatenate([q["tu"][:, P2:], q["v"]], axis=0),
                           jnp.concatenate([q["bh"], q["kh"]], axis=0), TN, P_B)
        for q in probs:
            s0 = st[q["j"]]
            yh = _dot(q["rm"], s0, NT, P_S) + q["y0"]
            y_scr[q["rows"], q["ls"]] = yh[:L] + yh[L:]
            st[q["j"]] = _dot(s0, q["mt"], NN, P_S) + q["cc"]
        return carry

    lax.fori_loop(0, nchunk // nsub, chunk, 0)

    y = y_scr[...]
    mean = head_sum(y) * (1.0 / N)
    yc = y - mean
    var = head_sum(yc * yc) * (1.0 / N)
    yn = yc * lax.rsqrt(var + RW_GN_EPS) * ln_g[...] + ln_b[...]
    o_ref[...] = (yn + bonus) * _silu(g_ref[...])


def _rwkv(proj, p, bsz, seq):
    tb = min(512, seq)
    spt = seq // tb
    row = lambda b, s: b * spt + s
    seg = lambda name: pl.BlockSpec((tb, WIDTH), lambda b, s, j=_blk512(name): (row(b, s), j))
    vec = lambda n: pl.BlockSpec((1, n), lambda b, s: (0, 0))
    mat = lambda m, n: pl.BlockSpec((m, n), lambda b, s: (0, 0))
    tile = lambda: pltpu.VMEM((tb, WIDTH), F32)
    return pl.pallas_call(
        _rwkv_kernel,
        grid=(bsz, spt),
        in_specs=[seg("rw_r"), seg("rw_k"), seg("rw_v"), seg("rw_gate"),
                  pl.BlockSpec((tb, LANE), lambda b, s, j=_blk128("rw_wa"): (row(b, s), j)),
                  vec(WIDTH), vec(WIDTH), vec(WIDTH), vec(LANE), vec(WIDTH), mat(LANE, WIDTH),
                  vec(WIDTH), mat(LANE, WIDTH), vec(WIDTH), vec(WIDTH), vec(WIDTH), vec(WIDTH),
                  vec(WIDTH), mat(WIDTH, WIDTH)],
        out_specs=pl.BlockSpec((tb, WIDTH), lambda b, s: (row(b, s), 0)),
        out_shape=jax.ShapeDtypeStruct((bsz * seq, WIDTH), F32),
        scratch_shapes=[pltpu.VMEM((SUBLANE, WIDTH), F32), pltpu.VMEM((SUBLANE, WIDTH), F32),
                        pltpu.VMEM((SUBLANE, WIDTH), F32), pltpu.VMEM((SUBLANE, LANE), F32),
                        pltpu.VMEM((RW_HEADS // 2, 2 * RW_DIM, 2 * RW_DIM), F32),
                        tile(), tile(), tile(), tile(), tile(), tile(), tile(), tile(), tile()],
        compiler_params=pltpu.CompilerParams(dimension_semantics=("arbitrary", "arbitrary")),
        name="rwkv7_mixer",
    )(proj, proj, proj, proj, proj,
      p["mu_r"], p["mu_k"], p["mu_v"], p["mu_wa"], p["w0"], p["w_up"], p["a0"], p["a_up"],
      p["k_k"], p["k_a"], p["r_k"], p["ln_g"], p["ln_b"], p["ones_bd"])


def _mamba_kernel(x_ref, bc_ref, z_ref, dt_ref,
                  cw_x, cw_bc, cb_x, cb_bc, dt_bias, a_log, d_skip, norm_w, expand,
                  o_ref,
                  car_x, car_bc, st, q_x, q_xdt, q_bc, q_acs, q_ax, y_scr):
    tb = x_ref.shape[0]
    L = M2_CHUNK
    nchunk = tb // L
    gw = WIDTH // M2_GROUPS
    hpg = M2_HEADS // M2_GROUPS

    @pl.when(pl.program_id(1) == 0)
    def _():
        car_x[...] = jnp.zeros_like(car_x)
        car_bc[...] = jnp.zeros_like(car_bc)
        st[...] = jnp.zeros_like(st)

    def conv(x_ref, car, w, b):
        x = x_ref[...]
        c = car[...]
        y = x * w[M2_CONV - 1:M2_CONV, :] + b[...]
        for j in range(1, M2_CONV):
            y = y + _shift_rows(x, c, j) * w[M2_CONV - 1 - j:M2_CONV - j, :]
        car[...] = x[tb - SUBLANE:, :]
        return _silu(y)

    xs = conv(x_ref, car_x, cw_x, cb_x)
    bcs = conv(bc_ref, car_bc, cw_bc, cb_bc)
    ex = expand[...]
    dt = _softplus(dt_ref[...] + dt_bias[...])
    a_cs = _chunk_cumsum(dt * (-jnp.exp(a_log[...])), L)
    q_acs[...] = a_cs
    q_ax[...] = _dot_exact_rhs(a_cs, ex, 3)
    q_x[...] = xs
    q_xdt[...] = xs * _dot_exact_rhs(dt, ex, 3)
    q_bc[...] = bcs

    tri_incl = _tri(L)

    def chunk(c, carry):
        rows = pl.ds(pl.multiple_of(c * L, L), L)
        a_c = q_acs[rows, :]
        a_t = a_c.T
        a_x = q_ax[rows, :]
        a_last = a_x[L - 1:L, :]
        xdt = q_xdt[rows, :]
        xdt_end = xdt * jnp.exp(a_last - a_x)
        e_ax = jnp.exp(a_x)
        e_last = jnp.exp(a_last)
        bcc = q_bc[rows, :]
        bms = [bcc[:, g * M2_STATE:(g + 1) * M2_STATE] for g in range(M2_GROUPS)]
        cms = [bcc[:, gw + g * M2_STATE:gw + (g + 1) * M2_STATE] for g in range(M2_GROUPS)]
        cbs = [_dot(cms[g], bms[g], NT, P_M2) for g in range(M2_GROUPS)]
        decs = []
        for h in range(M2_HEADS):
            diff = a_c[:, h:h + 1] - a_t[h:h + 1, :]
            decs.append(jnp.exp(jnp.where(tri_incl, diff, -jnp.inf)))
        y_diag = [_dot(cbs[h // hpg] * decs[h], xdt[:, h * M2_DIM:(h + 1) * M2_DIM], NN, P_M2)
                  for h in range(M2_HEADS)]
        kv = [_dot(bms[g], xdt_end[:, g * gw:(g + 1) * gw], TN, P_M2) for g in range(M2_GROUPS)]
        for g in range(M2_GROUPS):
            gs = slice(g * gw, (g + 1) * gw)
            s0 = st[g]
            y_off = _dot(cms[g], s0, NN, P_M2) * e_ax[:, gs]
            st[g] = s0 * e_last[:, gs] + kv[g]
            for hh in range(hpg):
                h = g * hpg + hh
                y_scr[rows, h * M2_DIM:(h + 1) * M2_DIM] = (
                    y_diag[h] + y_off[:, hh * M2_DIM:(hh + 1) * M2_DIM])
        return carry

    lax.fori_loop(0, nchunk, chunk, 0)

    y = (y_scr[...] + q_x[...] * d_skip[...]) * _silu(z_ref[...])
    outs = []
    for g in range(M2_GROUPS):
        yg = y[:, g * gw:(g + 1) * gw]
        ms = jnp.mean(yg * yg, axis=-1, keepdims=True)
        outs.append(yg * lax.rsqrt(ms + M2_EPS))
    o_ref[...] = jnp.concatenate(outs, axis=-1) * norm_w[...]


def _mamba(proj, p, bsz, seq):
    tb = min(512, seq)
    spt = seq // tb
    row = lambda b, s: b * spt + s
    seg = lambda name: pl.BlockSpec((tb, WIDTH), lambda b, s, j=_blk512(name): (row(b, s), j))
    vec = lambda n: pl.BlockSpec((1, n), lambda b, s: (0, 0))
    mat = lambda m, n: pl.BlockSpec((m, n), lambda b, s: (0, 0))
    tile = lambda n=WIDTH: pltpu.VMEM((tb, n), F32)
    return pl.pallas_call(
        _mamba_kernel,
        grid=(bsz, spt),
        in_specs=[seg("m2_x"), seg("m2_bc"), seg("m2_z"),
                  pl.BlockSpec((tb, LANE), lambda b, s, j=_blk128("m2_dt"): (row(b, s), j)),
                  mat(M2_CONV, WIDTH), mat(M2_CONV, WIDTH), vec(WIDTH), vec(WIDTH),
                  vec(LANE), vec(LANE), vec(WIDTH), vec(WIDTH), mat(LANE, WIDTH)],
        out_specs=pl.BlockSpec((tb, WIDTH), lambda b, s: (row(b, s), 0)),
        out_shape=jax.ShapeDtypeStruct((bsz * seq, WIDTH), F32),
        scratch_shapes=[pltpu.VMEM((SUBLANE, WIDTH), F32), pltpu.VMEM((SUBLANE, WIDTH), F32),
                        pltpu.VMEM((M2_GROUPS, M2_STATE, WIDTH // M2_GROUPS), F32),
                        tile(), tile(), tile(), tile(LANE), tile(), tile()],
        compiler_params=pltpu.CompilerParams(dimension_semantics=("arbitrary", "arbitrary")),
        name="mamba2_mixer",
    )(proj, proj, proj, proj,
      p["cw_x"], p["cw_bc"], p["cb_x"], p["cb_bc"], p["dt_bias"], p["a_log"], p["d_skip"],
      p["m2_norm_w"], p["expand"])


def _gla_kernel(qk_ref, v_ref, g_ref, gk_ref, gk_up, gk_b, norm_w,
                o_ref, st, q_g, q_qg, q_kg, o_scr):
    tb = qk_ref.shape[0]
    L = GLA_CHUNK
    nchunk = tb // L
    kd = GLA_HEADS * GLA_DK

    @pl.when(pl.program_id(1) == 0)
    def _():
        st[...] = jnp.zeros_like(st)

    la = -_softplus(-(_dot(gk_ref[...], gk_up[...], NN, 3) + gk_b[...])) * (1.0 / GLA_TAU)
    g = _chunk_cumsum(la, L)
    qk = qk_ref[...]
    q_g[...] = g
    q_qg[...] = qk[:, :kd] * (GLA_DK ** -0.5) * jnp.exp(g)
    q_kg[...] = qk[:, kd:] * jnp.exp(-g)

    tri_incl = _tri(L)
    nsub = GLA_SUBCHUNKS

    def chunk(it, carry):
        probs = []
        for i in range(nsub):
            rows = pl.ds(pl.multiple_of((it * nsub + i) * L, L), L)
            gc = q_g[rows, :]
            gl = gc[L - 1:L, :]
            kh = qk_ref[rows, kd:] * jnp.exp(gl - gc)
            dec = jnp.exp(gl)
            qg, kg, vc = q_qg[rows, :], q_kg[rows, :], v_ref[rows, :]
            for h in range(GLA_HEADS):
                ks = slice(h * GLA_DK, (h + 1) * GLA_DK)
                vs = slice(h * GLA_DV, (h + 1) * GLA_DV)
                probs.append(dict(rows=rows, h=h, vs=vs, qg=qg[:, ks], kg=kg[:, ks], v=vc[:, vs],
                                  kh=kh[:, ks], dec=dec[:, ks]))
        for q in probs:
            q["att"] = jnp.where(tri_incl, _dot(q["qg"], q["kg"], NT, P_GLA), 0.0)
        for q in probs:
            q["o"] = _dot(q["att"], q["v"], NN, P_GLA)
            q["kv"] = _dot(q["v"], q["kh"], TN, P_GLA)
        for q in probs:
            s0 = st[q["h"]]
            o_scr[q["rows"], q["vs"]] = q["o"] + _dot(q["qg"], s0, NT, P_GLA)
            st[q["h"]] = s0 * q["dec"] + q["kv"]
        return carry

    lax.fori_loop(0, nchunk // nsub, chunk, 0)

    o = o_scr[...]
    outs = []
    for h in range(GLA_HEADS):
        oh = o[:, h * GLA_DV:(h + 1) * GLA_DV]
        ms = jnp.mean(oh * oh, axis=-1, keepdims=True)
        outs.append(oh * lax.rsqrt(ms + GLA_EPS))
    o_ref[...] = jnp.concatenate(outs, axis=-1) * norm_w[...] * _silu(g_ref[...])


def _gla(proj, p, bsz, seq):
    tb = min(512, seq)
    spt = seq // tb
    row = lambda b, s: b * spt + s
    seg = lambda name: pl.BlockSpec((tb, WIDTH), lambda b, s, j=_blk512(name): (row(b, s), j))
    return pl.pallas_call(
        _gla_kernel,
        grid=(bsz, spt),
        in_specs=[seg("gla_qk"), seg("gla_v"), seg("gla_gate"),
                  pl.BlockSpec((tb, LANE), lambda b, s, j=_blk128("gla_gk"): (row(b, s), j)),
                  pl.BlockSpec((LANE, GLA_HEADS * GLA_DK), lambda b, s: (0, 0)),
                  pl.BlockSpec((1, GLA_HEADS * GLA_DK), lambda b, s: (0, 0)),
                  pl.BlockSpec((1, WIDTH), lambda b, s: (0, 0))],
        out_specs=pl.BlockSpec((tb, WIDTH), lambda b, s: (row(b, s), 0)),
        out_shape=jax.ShapeDtypeStruct((bsz * seq, WIDTH), F32),
        scratch_shapes=[pltpu.VMEM((GLA_HEADS, GLA_DV, GLA_DK), F32),
                        pltpu.VMEM((tb, GLA_HEADS * GLA_DK), F32),
                        pltpu.VMEM((tb, GLA_HEADS * GLA_DK), F32),
                        pltpu.VMEM((tb, GLA_HEADS * GLA_DK), F32),
                        pltpu.VMEM((tb, WIDTH), F32)],
        compiler_params=pltpu.CompilerParams(dimension_semantics=("arbitrary", "arbitrary")),
        name="gla_mixer",
    )(proj, proj, proj, proj, p["gk_up"], p["gk_b"], p["gla_norm_w"])


def _merge_kernel(alpha, y_rw, y_m2, y_gla, lg0, lg1, lg2, x_ref, mod_ref, wb, w_out, pg, pb,
                  o_ref):
    merged = (_sigmoid(lg0[...]) * _bdot(y_rw[...], wb[0])
              + _sigmoid(lg1[...]) * _bdot(y_m2[...], wb[1])
              + _sigmoid(lg2[...]) * _bdot(y_gla[...], wb[2]))
    y = _bdot(merged, w_out[...])
    res = alpha * x_ref[...] + (1.0 + mod_ref[2:3, :]) * y
    mu = jnp.mean(res, axis=-1, keepdims=True)
    rc = res - mu
    var = jnp.mean(rc * rc, axis=-1, keepdims=True)
    o_ref[...] = rc * lax.rsqrt(var + LN_EPS) * pg[...] + pb[...]


def _merge(y_rw, y_m2, y_gla, proj, x2, mod, p, seq, alpha):
    t, d = x2.shape
    tm = min(512, seq)
    tiles_per_seq = seq // tm
    ytile = lambda: pl.BlockSpec((tm, WIDTH), lambda i: (i, 0))
    logit = lambda j: pl.BlockSpec((tm, d), lambda i, j=j: (i, j))
    return pl.pallas_call(
        functools.partial(_merge_kernel, alpha),
        grid=(t // tm,),
        in_specs=[ytile(), ytile(), ytile(), logit(0), logit(1), logit(2),
                  pl.BlockSpec((tm, d), lambda i: (i, 0)),
                  pl.BlockSpec((None, 3, d), lambda i: (i // tiles_per_seq, 0, 0)),
                  pl.BlockSpec((N_BRANCH, WIDTH, d), lambda i: (0, 0, 0)),
                  pl.BlockSpec((d, d), lambda i: (0, 0)),
                  pl.BlockSpec((1, d), lambda i: (0, 0)),
                  pl.BlockSpec((1, d), lambda i: (0, 0))],
        out_specs=pl.BlockSpec((tm, d), lambda i: (i, 0)),
        out_shape=jax.ShapeDtypeStruct((t, d), F32),
        compiler_params=pltpu.CompilerParams(dimension_semantics=("parallel",)),
        name="merge_out",
    )(y_rw, y_m2, y_gla, proj, proj, proj, x2, mod, p["w_branch"], p["w_out"], p["post_g"],
      p["post_b"])


def _pad_cols(a, n):
    return jnp.pad(a, ((0, 0), (0, n - a.shape[1])))


def _pad_rows_at(a, start, n):
    return jnp.pad(a, ((start, n - start - a.shape[0]), (0, 0)))


def _layer_params(l, w_in, rw_mu, rw_w0, rw_w_up, rw_a0, rw_a_up, rw_k_k, rw_k_a, rw_r_k,
                  rw_ln_g, rw_ln_b, m2_conv_w, m2_conv_b, m2_dt_bias, m2_a_log, m2_d_skip,
                  m2_norm_w, gla_gk_up, gla_gk_b, gla_norm_w, w_branch, w_out, post_g, post_b):
    W = WIDTH
    rank2 = 2 * RW_RANK
    sizes = (3 * W + rank2, W, W + 2 * M2_GROUPS * M2_STATE, M2_HEADS, W,
             GLA_HEADS * GLA_DK, GLA_HEADS * GLA_DK, GLA_HEADS * GLA_DV, GLA_RANK, W,
             N_BRANCH * D_MODEL)
    offs = [0]
    for s in sizes:
        offs.append(offs[-1] + s)
    (o_rw, o_rwg, o_xbc, o_dt, o_z, o_q, o_k, o_v, o_gk, o_gg, o_merge, _) = offs
    wl = w_in[l]
    cols = lambda a, n: wl[:, a:a + n]
    segs = {
        "rw_r": cols(o_rw, W), "rw_k": cols(o_rw + W, W), "rw_v": cols(o_rw + 2 * W, W),
        "rw_gate": cols(o_rwg, W), "m2_x": cols(o_xbc, W), "m2_bc": cols(o_xbc + W, W),
        "m2_z": cols(o_z, W), "gla_qk": cols(o_q, W), "gla_v": cols(o_v, W),
        "gla_gate": cols(o_gg, W),
        "rw_wa": cols(o_rw + 3 * W, rank2), "m2_dt": _pad_cols(cols(o_dt, M2_HEADS), LANE),
        "gla_gk": _pad_cols(cols(o_gk, GLA_RANK), LANE),
    }
    w_cat = jnp.concatenate([cols(o_merge, N_BRANCH * D_MODEL)] + [segs[n] for n in SEG512]
                            + [segs[n] for n in SEG128], axis=1)
    w_cat = _pad_cols(w_cat, PROJ_WIDTH).astype(BF16)

    row = lambda a: a.reshape(1, -1)
    mu = rw_mu[l]
    head_of_ch = jnp.arange(W) // RW_DIM
    ones_bd = (head_of_ch[:, None] == head_of_ch[None, :]).astype(BF16)
    expand = (jnp.arange(LANE)[:, None] == (jnp.arange(W) // M2_DIM)[None, :]).astype(BF16)
    cw = m2_conv_w[l]
    cb = m2_conv_b[l]
    return {
        "w_in": w_cat,
        "mu_r": row(mu[:W]), "mu_k": row(mu[W:2 * W]), "mu_v": row(mu[2 * W:3 * W]),
        "mu_wa": row(mu[3 * W:]),
        "w0": row(rw_w0[l]), "w_up": _pad_rows_at(rw_w_up[l], 0, LANE),
        "a0": row(rw_a0[l]), "a_up": _pad_rows_at(rw_a_up[l], RW_RANK, LANE),
        "k_k": row(rw_k_k[l]), "k_a": row(rw_k_a[l]), "r_k": row(rw_r_k[l]),
        "ln_g": row(rw_ln_g[l]), "ln_b": row(rw_ln_b[l]), "ones_bd": ones_bd,
        "cw_x": cw[:, :W], "cw_bc": cw[:, W:], "cb_x": row(cb[:W]), "cb_bc": row(cb[W:]),
        "dt_bias": _pad_cols(row(m2_dt_bias[l]), LANE), "a_log": _pad_cols(row(m2_a_log[l]), LANE),
        "d_skip": row(jnp.repeat(m2_d_skip[l], M2_DIM)), "m2_norm_w": row(m2_norm_w[l]),
        "expand": expand,
        "gk_up": _pad_rows_at(gla_gk_up[l], 0, LANE), "gk_b": row(gla_gk_b[l]),
        "gla_norm_w": row(jnp.tile(gla_norm_w[l], GLA_HEADS)),
        "w_branch": w_branch[l].astype(BF16), "w_out": w_out[l].astype(BF16),
        "post_g": row(post_g[l]), "post_b": row(post_b[l]),
    }


def kernel(x, c, ada_w, ada_b, w_in, rw_mu, rw_w0, rw_w_up, rw_a0, rw_a_up, rw_k_k, rw_k_a, rw_r_k, rw_ln_g, rw_ln_b, m2_conv_w, m2_conv_b, m2_dt_bias, m2_a_log, m2_d_skip, m2_norm_w, gla_gk_up, gla_gk_b, gla_norm_w, w_branch, w_out, post_g, post_b):
    bsz, seq, d = x.shape
    depth = ada_w.shape[0]
    alpha = (2.0 * depth) ** 0.25
    mod_all = _adaln(c, ada_w, ada_b).reshape(depth, bsz, 3, d)
    x2 = x.reshape(bsz * seq, d)
    for l in range(depth):
        p = _layer_params(l, w_in, rw_mu, rw_w0, rw_w_up, rw_a0, rw_a_up, rw_k_k, rw_k_a,
                          rw_r_k, rw_ln_g, rw_ln_b, m2_conv_w, m2_conv_b, m2_dt_bias, m2_a_log,
                          m2_d_skip, m2_norm_w, gla_gk_up, gla_gk_b, gla_norm_w, w_branch, w_out,
                          post_g, post_b)
        mod = mod_all[l]
        proj = _inproj(x2, mod, p["w_in"], seq)
        y_rw = _rwkv(proj, p, bsz, seq)
        y_m2 = _mamba(proj, p, bsz, seq)
        y_gla = _gla(proj, p, bsz, seq)
        x2 = _merge(y_rw, y_m2, y_gla, proj, x2, mod, p, seq, alpha)
    return x2.reshape(bsz, seq, d)
```

```python
import functools

import jax
import jax.numpy as jnp
from jax import lax
from jax.experimental import pallas as pl
from jax.experimental.pallas import tpu as pltpu

F32 = jnp.float32
BF16 = jnp.bfloat16
HIGHEST = lax.Precision.HIGHEST

D_MODEL = 1024
WIDTH = 512
LN_EPS = 1e-5
RW_HEADS, RW_DIM, RW_RANK = 8, 64, 64
RW_GN_EPS = 64e-5
RW_CHUNK = 64
RW_SUBCHUNKS = 4
M2_HEADS, M2_DIM, M2_GROUPS, M2_STATE, M2_CONV = 8, 64, 2, 128, 4
M2_CHUNK = 128
M2_EPS = 1e-5
GLA_HEADS, GLA_DK, GLA_DV, GLA_RANK = 4, 64, 128, 16
GLA_TAU = 16.0
GLA_CHUNK = 64
GLA_SUBCHUNKS = 2
GLA_EPS = 1e-5
N_BRANCH = 3

LANE = 128
SUBLANE = 8

SEG512 = ("rw_r", "rw_k", "rw_v", "rw_gate", "m2_x", "m2_bc", "m2_z", "gla_qk", "gla_v", "gla_gate")
SEG128 = ("rw_wa", "m2_dt", "gla_gk")
COL512_BASE = N_BRANCH * D_MODEL
COL128_BASE = COL512_BASE + WIDTH * len(SEG512)
PROJ_WIDTH = -(-(COL128_BASE + LANE * len(SEG128)) // WIDTH) * WIDTH
PROJ_COL_TILES = 4
PROJ_TILE_N = PROJ_WIDTH // PROJ_COL_TILES
assert PROJ_TILE_N * PROJ_COL_TILES == PROJ_WIDTH and PROJ_TILE_N % LANE == 0


def _blk512(name):
    return (COL512_BASE + WIDTH * SEG512.index(name)) // WIDTH


def _blk128(name):
    return (COL128_BASE + LANE * SEG128.index(name)) // LANE


def _mm(a, b, precision=HIGHEST):
    return lax.dot_general(a, b, (((1,), (0,)), ((), ())), precision=precision,
                           preferred_element_type=F32)


def _mm_nt(a, b, precision=HIGHEST):
    return lax.dot_general(a, b, (((1,), (1,)), ((), ())), precision=precision,
                           preferred_element_type=F32)


def _mm_tn(a, b, precision=HIGHEST):
    return lax.dot_general(a, b, (((0,), (0,)), ((), ())), precision=precision,
                           preferred_element_type=F32)


def _bdot(a, b):
    return jnp.dot(a.astype(BF16), b.astype(BF16), preferred_element_type=F32)


P_A = 1
P_INV = 1
P_B = 1
P_S = 1
P_M2 = 1
P_GLA = 1

NN = ((1,), (0,))
NT = ((1,), (1,))
TN = ((0,), (0,))


def _dg(a, b, dims):
    return lax.dot_general(a, b, (dims, ((), ())), preferred_element_type=F32)


def _split2(x):
    hi = x.astype(BF16)
    return hi, (x - hi.astype(F32)).astype(BF16)


def _dot(a, b, dims=NN, passes=1):
    if passes == 6:
        return lax.dot_general(a, b, (dims, ((), ())), precision=HIGHEST,
                               preferred_element_type=F32)
    if passes == 1:
        return _dg(a.astype(BF16), b.astype(BF16), dims)
    a_hi, a_lo = _split2(a)
    b_hi, b_lo = _split2(b)
    return _dg(a_hi, b_hi, dims) + _dg(a_lo, b_hi, dims) + _dg(a_hi, b_lo, dims)


def _dot_exact_rhs(a, b_bf16, pieces=2):
    out = None
    for _ in range(pieces):
        piece = a.astype(BF16)
        a = a - piece.astype(F32)
        term = _dg(piece, b_bf16, NN)
        out = term if out is None else out + term
    return out


def _chunk_cumsum(x, chunk):
    pos = lax.broadcasted_iota(jnp.int32, x.shape, 0) & (chunk - 1)
    step = 1
    while step < chunk:
        x = x + jnp.where(pos >= step, pltpu.roll(x, step, 0), 0.0)
        step *= 2
    return x


def _sigmoid(x):
    return jax.nn.sigmoid(x)


def _silu(x):
    return x * jax.nn.sigmoid(x)


def _softplus(x):
    return jnp.maximum(x, 0.0) + jnp.log1p(jnp.exp(-jnp.abs(x)))


def _tri(n, strict=False):
    r = lax.broadcasted_iota(jnp.int32, (n, n), 0)
    c = lax.broadcasted_iota(jnp.int32, (n, n), 1)
    return (r > c) if strict else (r >= c)


def _shift_rows(x, carry, j):
    ext = jnp.concatenate([carry, x], axis=0)
    return pltpu.roll(ext, j, 0)[SUBLANE:, :]


def _adaln_kernel(c_ref, w_ref, b_ref, o_ref):
    o_ref[...] = _mm(_silu(c_ref[...]), w_ref[...]) + b_ref[...]


def _adaln(c, ada_w, ada_b):
    depth, d, n = ada_w.shape
    bsz = c.shape[0]
    tn = 768
    return pl.pallas_call(
        _adaln_kernel,
        grid=(depth, n // tn),
        in_specs=[pl.BlockSpec((bsz, d), lambda l, j: (0, 0)),
                  pl.BlockSpec((None, d, tn), lambda l, j: (l, 0, j)),
                  pl.BlockSpec((None, 1, tn), lambda l, j: (l, 0, j))],
        out_specs=pl.BlockSpec((None, bsz, tn), lambda l, j: (l, 0, j)),
        out_shape=jax.ShapeDtypeStruct((depth, bsz, n), F32),
        name="adaln_mod",
    )(c, ada_w, ada_b.reshape(depth, 1, n))


def _inproj_kernel(x_ref, mod_ref, w_ref, o_ref, h_scr):
    @pl.when(pl.program_id(1) == 0)
    def _():
        x = x_ref[...]
        mu = jnp.mean(x, axis=-1, keepdims=True)
        xc = x - mu
        var = jnp.mean(xc * xc, axis=-1, keepdims=True)
        h = xc * lax.rsqrt(var + LN_EPS) * (1.0 + mod_ref[1:2, :]) + mod_ref[0:1, :]
        h_scr[...] = h.astype(BF16)

    o_ref[...] = jnp.dot(h_scr[...], w_ref[...], preferred_element_type=F32)


def _inproj(x2, mod, w_bf16, seq):
    t, d = x2.shape
    n = w_bf16.shape[1]
    tm = min(1024, seq)
    tiles_per_seq = seq // tm
    return pl.pallas_call(
        _inproj_kernel,
        grid=(t // tm, n // PROJ_TILE_N),
        in_specs=[pl.BlockSpec((tm, d), lambda i, j: (i, 0)),
                  pl.BlockSpec((None, 3, d), lambda i, j: (i // tiles_per_seq, 0, 0)),
                  pl.BlockSpec((d, PROJ_TILE_N), lambda i, j: (0, j))],
        out_specs=pl.BlockSpec((tm, PROJ_TILE_N), lambda i, j: (i, j)),
        out_shape=jax.ShapeDtypeStruct((t, n), F32),
        scratch_shapes=[pltpu.VMEM((tm, d), BF16)],
        compiler_params=pltpu.CompilerParams(dimension_semantics=("parallel", "arbitrary")),
        name="inproj",
    )(x2, mod, w_bf16)


def _rwkv_kernel(r_ref, k_ref, v_ref, g_ref, wa_ref,
                 mu_r, mu_k, mu_v, mu_wa, w0, w_up, a0, a_up, k_k, k_a, r_k, ln_g, ln_b, ones_bd,
                 o_ref,
                 car_r, car_k, car_v, car_wa, st, q_at, q_bt, q_kt, q_rt, q_b, q_k, q_v, q_g,
                 y_scr):
    tb = r_ref.shape[0]
    nchunk = tb // RW_CHUNK
    L, N = RW_CHUNK, RW_DIM

    @pl.when(pl.program_id(1) == 0)
    def _():
        car_r[...] = jnp.zeros_like(car_r)
        car_k[...] = jnp.zeros_like(car_k)
        car_v[...] = jnp.zeros_like(car_v)
        car_wa[...] = jnp.zeros_like(car_wa)
        st[...] = jnp.zeros_like(st)

    def lerp(x_ref, car, mu):
        x = x_ref[...]
        sh = _shift_rows(x, car[...], 1)
        car[...] = x[tb - SUBLANE:, :]
        return x + mu[...] * (sh - x)

    r = lerp(r_ref, car_r, mu_r)
    k = lerp(k_ref, car_k, mu_k)
    v = lerp(v_ref, car_v, mu_v)
    wa = lerp(wa_ref, car_wa, mu_wa)
    ones = ones_bd[...]
    head_sum = lambda t: _dot_exact_rhs(t, ones)

    w = w0[...] + _dot(jnp.tanh(wa), w_up[...], NN, 3)
    lw = -0.6065306597126334 * _sigmoid(w)
    a = _sigmoid(a0[...] + _dot(wa, a_up[...], NN, 3))
    kk = k * k_k[...]
    kk = kk * lax.rsqrt(jnp.maximum(head_sum(kk * kk), 1e-24))
    k2 = k * (1.0 + (a - 1.0) * k_a[...])
    bonus = head_sum(r * k2 * r_k[...]) * v
    b = kk * a

    g = _chunk_cumsum(lw, L)
    eng = jnp.exp(-g)
    q_at[...] = -kk * jnp.exp(g - lw)
    q_bt[...] = b * eng
    q_kt[...] = k2 * eng
    q_rt[...] = r * jnp.exp(g)
    q_b[...] = b
    q_k[...] = k2
    q_v[...] = v
    q_g[...] = g

    P2 = 2 * L
    npair = RW_HEADS // 2
    lane_lo = lax.broadcasted_iota(jnp.int32, (L, P2), 1) < N

    def hat(x):
        return jnp.concatenate([jnp.where(lane_lo, x, 0.0), jnp.where(lane_lo, 0.0, x)], axis=0)

    ri = lax.broadcasted_iota(jnp.int32, (2 * P2, 2 * P2), 0)
    ci = lax.broadcasted_iota(jnp.int32, (2 * P2, 2 * P2), 1)
    same_head = ((ri & (P2 - 1)) >= L) == ((ci & (P2 - 1)) >= L)
    tpos = ri & (L - 1)
    spos = ci & (L - 1)
    mask4 = same_head & (jnp.where(ri < P2, tpos, tpos + 1) > spos)
    eye = (lax.broadcasted_iota(jnp.int32, (P2, P2), 0)
           == lax.broadcasted_iota(jnp.int32, (P2, P2), 1)).astype(F32)
    nsub = RW_SUBCHUNKS

    def chunk(it, carry):
        probs = []
        for i in range(nsub):
            rows = pl.ds(pl.multiple_of((it * nsub + i) * L, L), L)
            gc = q_g[rows, :]
            gl = gc[L - 1:L, :]
            egl = jnp.exp(gl - gc)
            dec = jnp.exp(gl)
            bh = q_b[rows, :] * egl
            kh = q_k[rows, :] * egl
            for j in range(npair):
                ls = slice(j * P2, (j + 1) * P2)
                probs.append(dict(
                    rows=rows, ls=ls, j=j, dec=dec[:, ls],
                    a=hat(q_at[rows, ls]), r=hat(q_rt[rows, ls]), b=hat(q_bt[rows, ls]),
                    k=hat(q_kt[rows, ls]), v=hat(q_v[rows, ls]), bh=hat(bh[:, ls]),
                    kh=hat(kh[:, ls])))
        for q in probs:
            m = _dot(jnp.concatenate([q["a"], q["r"]], axis=0),
                     jnp.concatenate([q["b"], q["k"]], axis=0), NT, P_A)
            m = jnp.where(mask4, m, 0.0)
            q["a_ab"], q["a_ak"] = m[:P2, :P2], m[:P2, P2:]
            q["a_rb"], q["a_rk"] = m[P2:, :P2], m[P2:, P2:]
            q["p"] = q["a_ab"]
            q["t"] = eye + q["a_ab"]
        for q in probs:
            q["p"] = _dot(q["p"], q["p"], NN, P_INV)
        for _ in range(4):
            for q in probs:
                tp = _dot(q["p"], jnp.concatenate([q["t"], q["p"]], axis=1), NN, P_INV)
                q["t"], q["p"] = q["t"] + tp[:, :P2], tp[:, P2:]
        for q in probs:
            q["t"] = q["t"] + _dot(q["p"], q["t"], NN, P_INV)
        for q in probs:
            q["av"] = _dot(q["a_ak"], q["v"], NN, P_B)
        for q in probs:
            q["tu"] = _dot(q["t"], jnp.concatenate([q["a"], q["av"]], axis=1), NN, P_B)
        for q in probs:
            lhs = jnp.concatenate([q["a_rb"], q["a_rk"]], axis=1)
            rhs = jnp.concatenate(
                [q["tu"], jnp.concatenate([jnp.zeros_like(q["v"]), q["v"]], axis=1)], axis=0)
            ru = _dot(lhs, rhs, NN, P_B)
            q["rm"] = q["r"] + ru[:, :P2]
            q["y0"] = ru[:, P2:]
        for q in probs:
            q["mt"] = eye * q["dec"] + _dot(q["tu"][:, :P2], q["bh"], TN, P_B)
            q["cc"] = _dot(jnp.concatenate([q["tu"][:, P2:], q["v"]], axis=0),
                           jnp.concatenate([q["bh"], q["kh"]], axis=0), TN, P_B)
        for q in probs:
            s0 = st[q["j"]]
            yh = _dot(q["rm"], s0, NT, P_S) + q["y0"]
            y_scr[q["rows"], q["ls"]] = yh[:L] + yh[L:]
            st[q["j"]] = _dot(s0, q["mt"], NN, P_S) + q["cc"]
        return carry

    lax.fori_loop(0, nchunk // nsub, chunk, 0)

    y = y_scr[...]
    mean = head_sum(y) * (1.0 / N)
    yc = y - mean
    var = head_sum(yc * yc) * (1.0 / N)
    yn = yc * lax.rsqrt(var + RW_GN_EPS) * ln_g[...] + ln_b[...]
    o_ref[...] = (yn + bonus) * _silu(g_ref[...])


def _rwkv(proj, p, bsz, seq):
    tb = min(512, seq)
    spt = seq // tb
    row = lambda b, s: b * spt + s
    seg = lambda name: pl.BlockSpec((tb, WIDTH), lambda b, s, j=_blk512(name): (row(b, s), j))
    vec = lambda n: pl.BlockSpec((1, n), lambda b, s: (0, 0))
    mat = lambda m, n: pl.BlockSpec((m, n), lambda b, s: (0, 0))
    tile = lambda: pltpu.VMEM((tb, WIDTH), F32)
    return pl.pallas_call(
        _rwkv_kernel,
        grid=(bsz, spt),
        in_specs=[seg("rw_r"), seg("rw_k"), seg("rw_v"), seg("rw_gate"),
                  pl.BlockSpec((tb, LANE), lambda b, s, j=_blk128("rw_wa"): (row(b, s), j)),
                  vec(WIDTH), vec(WIDTH), vec(WIDTH), vec(LANE), vec(WIDTH), mat(LANE, WIDTH),
                  vec(WIDTH), mat(LANE, WIDTH), vec(WIDTH), vec(WIDTH), vec(WIDTH), vec(WIDTH),
                  vec(WIDTH), mat(WIDTH, WIDTH)],
        out_specs=pl.BlockSpec((tb, WIDTH), lambda b, s: (row(b, s), 0)),
        out_shape=jax.ShapeDtypeStruct((bsz * seq, WIDTH), F32),
        scratch_shapes=[pltpu.VMEM((SUBLANE, WIDTH), F32), pltpu.VMEM((SUBLANE, WIDTH), F32),
                        pltpu.VMEM((SUBLANE, WIDTH), F32), pltpu.VMEM((SUBLANE, LANE), F32),
                        pltpu.VMEM((RW_HEADS // 2, 2 * RW_DIM, 2 * RW_DIM), F32),
                        tile(), tile(), tile(), tile(), tile(), tile(), tile(), tile(), tile()],
        compiler_params=pltpu.CompilerParams(dimension_semantics=("arbitrary", "arbitrary")),
        name="rwkv7_mixer",
    )(proj, proj, proj, proj, proj,
      p["mu_r"], p["mu_k"], p["mu_v"], p["mu_wa"], p["w0"], p["w_up"], p["a0"], p["a_up"],
      p["k_k"], p["k_a"], p["r_k"], p["ln_g"], p["ln_b"], p["ones_bd"])


def _mamba_kernel(x_ref, bc_ref, z_ref, dt_ref,
                  cw_x, cw_bc, cb_x, cb_bc, dt_bias, a_log, d_skip, norm_w, expand,
                  o_ref,
                  car_x, car_bc, st, q_x, q_xdt, q_bc, q_acs, q_ax, y_scr):
    tb = x_ref.shape[0]
    L = M2_CHUNK
    nchunk = tb // L
    gw = WIDTH // M2_GROUPS
    hpg = M2_HEADS // M2_GROUPS

    @pl.when(pl.program_id(1) == 0)
    def _():
        car_x[...] = jnp.zeros_like(car_x)
        car_bc[...] = jnp.zeros_like(car_bc)
        st[...] = jnp.zeros_like(st)

    def conv(x_ref, car, w, b):
        x = x_ref[...]
        c = car[...]
        y = x * w[M2_CONV - 1:M2_CONV, :] + b[...]
        for j in range(1, M2_CONV):
            y = y + _shift_rows(x, c, j) * w[M2_CONV - 1 - j:M2_CONV - j, :]
        car[...] = x[tb - SUBLANE:, :]
        return _silu(y)

    xs = conv(x_ref, car_x, cw_x, cb_x)
    bcs = conv(bc_ref, car_bc, cw_bc, cb_bc)
    ex = expand[...]
    dt = _softplus(dt_ref[...] + dt_bias[...])
    a_cs = _chunk_cumsum(dt * (-jnp.exp(a_log[...])), L)
    q_acs[...] = a_cs
    q_ax[...] = _dot_exact_rhs(a_cs, ex, 3)
    q_x[...] = xs
    q_xdt[...] = xs * _dot_exact_rhs(dt, ex, 3)
    q_bc[...] = bcs

    tri_incl = _tri(L)

    def chunk(c, carry):
        rows = pl.ds(pl.multiple_of(c * L, L), L)
        a_c = q_acs[rows, :]
        a_t = a_c.T
        a_x = q_ax[rows, :]
        a_last = a_x[L - 1:L, :]
        xdt = q_xdt[rows, :]
        xdt_end = xdt * jnp.exp(a_last - a_x)
        e_ax = jnp.exp(a_x)
        e_last = jnp.exp(a_last)
        bcc = q_bc[rows, :]
        bms = [bcc[:, g * M2_STATE:(g + 1) * M2_STATE] for g in range(M2_GROUPS)]
        cms = [bcc[:, gw + g * M2_STATE:gw + (g + 1) * M2_STATE] for g in range(M2_GROUPS)]
        cbs = [_dot(cms[g], bms[g], NT, P_M2) for g in range(M2_GROUPS)]
        decs = []
        for h in range(M2_HEADS):
            diff = a_c[:, h:h + 1] - a_t[h:h + 1, :]
            decs.append(jnp.exp(jnp.where(tri_incl, diff, -jnp.inf)))
        y_diag = [_dot(cbs[h // hpg] * decs[h], xdt[:, h * M2_DIM:(h + 1) * M2_DIM], NN, P_M2)
                  for h in range(M2_HEADS)]
        kv = [_dot(bms[g], xdt_end[:, g * gw:(g + 1) * gw], TN, P_M2) for g in range(M2_GROUPS)]
        for g in range(M2_GROUPS):
            gs = slice(g * gw, (g + 1) * gw)
            s0 = st[g]
            y_off = _dot(cms[g], s0, NN, P_M2) * e_ax[:, gs]
            st[g] = s0 * e_last[:, gs] + kv[g]
            for hh in range(hpg):
                h = g * hpg + hh
                y_scr[rows, h * M2_DIM:(h + 1) * M2_DIM] = (
                    y_diag[h] + y_off[:, hh * M2_DIM:(hh + 1) * M2_DIM])
        return carry

    lax.fori_loop(0, nchunk, chunk, 0)

    y = (y_scr[...] + q_x[...] * d_skip[...]) * _silu(z_ref[...])
    outs = []
    for g in range(M2_GROUPS):
        yg = y[:, g * gw:(g + 1) * gw]
        ms = jnp.mean(yg * yg, axis=-1, keepdims=True)
        outs.append(yg * lax.rsqrt(ms + M2_EPS))
    o_ref[...] = jnp.concatenate(outs, axis=-1) * norm_w[...]


def _mamba(proj, p, bsz, seq):
    tb = min(512, seq)
    spt = seq // tb
    row = lambda b, s: b * spt + s
    seg = lambda name: pl.BlockSpec((tb, WIDTH), lambda b, s, j=_blk512(name): (row(b, s), j))
    vec = lambda n: pl.BlockSpec((1, n), lambda b, s: (0, 0))
    mat = lambda m, n: pl.BlockSpec((m, n), lambda b, s: (0, 0))
    tile = lambda n=WIDTH: pltpu.VMEM((tb, n), F32)
    return pl.pallas_call(
        _mamba_kernel,
        grid=(bsz, spt),
        in_specs=[seg("m2_x"), seg("m2_bc"), seg("m2_z"),
                  pl.BlockSpec((tb, LANE), lambda b, s, j=_blk128("m2_dt"): (row(b, s), j)),
                  mat(M2_CONV, WIDTH), mat(M2_CONV, WIDTH), vec(WIDTH), vec(WIDTH),
                  vec(LANE), vec(LANE), vec(WIDTH), vec(WIDTH), mat(LANE, WIDTH)],
        out_specs=pl.BlockSpec((tb, WIDTH), lambda b, s: (row(b, s), 0)),
        out_shape=jax.ShapeDtypeStruct((bsz * seq, WIDTH), F32),
        scratch_shapes=[pltpu.VMEM((SUBLANE, WIDTH), F32), pltpu.VMEM((SUBLANE, WIDTH), F32),
                        pltpu.VMEM((M2_GROUPS, M2_STATE, WIDTH // M2_GROUPS), F32),
                        tile(), tile(), tile(), tile(LANE), tile(), tile()],
        compiler_params=pltpu.CompilerParams(dimension_semantics=("arbitrary", "arbitrary")),
        name="mamba2_mixer",
    )(proj, proj, proj, proj,
      p["cw_x"], p["cw_bc"], p["cb_x"], p["cb_bc"], p["dt_bias"], p["a_log"], p["d_skip"],
      p["m2_norm_w"], p["expand"])


def _gla_kernel(qk_ref, v_ref, g_ref, gk_ref, gk_up, gk_b, norm_w,
                o_ref, st, q_g, q_qg, q_kg, o_scr):
    tb = qk_ref.shape[0]
    L = GLA_CHUNK
    nchunk = tb // L
    kd = GLA_HEADS * GLA_DK

    @pl.when(pl.program_id(1) == 0)
    def _():
        st[...] = jnp.zeros_like(st)

    la = -_softplus(-(_dot(gk_ref[...], gk_up[...], NN, 3) + gk_b[...])) * (1.0 / GLA_TAU)
    g = _chunk_cumsum(la, L)
    qk = qk_ref[...]
    q_g[...] = g
    q_qg[...] = qk[:, :kd] * (GLA_DK ** -0.5) * jnp.exp(g)
    q_kg[...] = qk[:, kd:] * jnp.exp(-g)

    tri_incl = _tri(L)
    nsub = GLA_SUBCHUNKS

    def chunk(it, carry):
        probs = []
        for i in range(nsub):
            rows = pl.ds(pl.multiple_of((it * nsub + i) * L, L), L)
            gc = q_g[rows, :]
            gl = gc[L - 1:L, :]
            kh = qk_ref[rows, kd:] * jnp.exp(gl - gc)
            dec = jnp.exp(gl)
            qg, kg, vc = q_qg[rows, :], q_kg[rows, :], v_ref[rows, :]
            for h in range(GLA_HEADS):
                ks = slice(h * GLA_DK, (h + 1) * GLA_DK)
                vs = slice(h * GLA_DV, (h + 1) * GLA_DV)
                probs.append(dict(rows=rows, h=h, vs=vs, qg=qg[:, ks], kg=kg[:, ks], v=vc[:, vs],
                                  kh=kh[:, ks], dec=dec[:, ks]))
        for q in probs:
            q["att"] = jnp.where(tri_incl, _dot(q["qg"], q["kg"], NT, P_GLA), 0.0)
        for q in probs:
            q["o"] = _dot(q["att"], q["v"], NN, P_GLA)
            q["kv"] = _dot(q["v"], q["kh"], TN, P_GLA)
        for q in probs:
            s0 = st[q["h"]]
            o_scr[q["rows"], q["vs"]] = q["o"] + _dot(q["qg"], s0, NT, P_GLA)
            st[q["h"]] = s0 * q["dec"] + q["kv"]
        return carry

    lax.fori_loop(0, nchunk // nsub, chunk, 0)

    o = o_scr[...]
    outs = []
    for h in range(GLA_HEADS):
        oh = o[:, h * GLA_DV:(h + 1) * GLA_DV]
        ms = jnp.mean(oh * oh, axis=-1, keepdims=True)
        outs.append(oh * lax.rsqrt(ms + GLA_EPS))
    o_ref[...] = jnp.concatenate(outs, axis=-1) * norm_w[...] * _silu(g_ref[...])


def _gla(proj, p, bsz, seq):
    tb = min(512, seq)
    spt = seq // tb
    row = lambda b, s: b * spt + s
    seg = lambda name: pl.BlockSpec((tb, WIDTH), lambda b, s, j=_blk512(name): (row(b, s), j))
    return pl.pallas_call(
        _gla_kernel,
        grid=(bsz, spt),
        in_specs=[seg("gla_qk"), seg("gla_v"), seg("gla_gate"),
                  pl.BlockSpec((tb, LANE), lambda b, s, j=_blk128("gla_gk"): (row(b, s), j)),
                  pl.BlockSpec((LANE, GLA_HEADS * GLA_DK), lambda b, s: (0, 0)),
                  pl.BlockSpec((1, GLA_HEADS * GLA_DK), lambda b, s: (0, 0)),
                  pl.BlockSpec((1, WIDTH), lambda b, s: (0, 0))],
        out_specs=pl.BlockSpec((tb, WIDTH), lambda b, s: (row(b, s), 0)),
        out_shape=jax.ShapeDtypeStruct((bsz * seq, WIDTH), F32),
        scratch_shapes=[pltpu.VMEM((GLA_HEADS, GLA_DV, GLA_DK), F32),
                        pltpu.VMEM((tb, GLA_HEADS * GLA_DK), F32),
                        pltpu.VMEM((tb, GLA_HEADS * GLA_DK), F32),
                        pltpu.VMEM((tb, GLA_HEADS * GLA_DK), F32),
                        pltpu.VMEM((tb, WIDTH), F32)],
        compiler_params=pltpu.CompilerParams(dimension_semantics=("arbitrary", "arbitrary")),
        name="gla_mixer",
    )(proj, proj, proj, proj, p["gk_up"], p["gk_b"], p["gla_norm_w"])


def _merge_kernel(alpha, y_rw, y_m2, y_gla, lg0, lg1, lg2, x_ref, mod_ref, wb, w_out, pg, pb,
                  o_ref):
    merged = (_sigmoid(lg0[...]) * _bdot(y_rw[...], wb[0])
              + _sigmoid(lg1[...]) * _bdot(y_m2[...], wb[1])
              + _sigmoid(lg2[...]) * _bdot(y_gla[...], wb[2]))
    y = _bdot(merged, w_out[...])
    res = alpha * x_ref[...] + (1.0 + mod_ref[2:3, :]) * y
    mu = jnp.mean(res, axis=-1, keepdims=True)
    rc = res - mu
    var = jnp.mean(rc * rc, axis=-1, keepdims=True)
    o_ref[...] = rc * lax.rsqrt(var + LN_EPS) * pg[...] + pb[...]


def _merge(y_rw, y_m2, y_gla, proj, x2, mod, p, seq, alpha):
    t, d = x2.shape
    tm = min(512, seq)
    tiles_per_seq = seq // tm
    ytile = lambda: pl.BlockSpec((tm, WIDTH), lambda i: (i, 0))
    logit = lambda j: pl.BlockSpec((tm, d), lambda i, j=j: (i, j))
    return pl.pallas_call(
        functools.partial(_merge_kernel, alpha),
        grid=(t // tm,),
        in_specs=[ytile(), ytile(), ytile(), logit(0), logit(1), logit(2),
                  pl.BlockSpec((tm, d), lambda i: (i, 0)),
                  pl.BlockSpec((None, 3, d), lambda i: (i // tiles_per_seq, 0, 0)),
                  pl.BlockSpec((N_BRANCH, WIDTH, d), lambda i: (0, 0, 0)),
                  pl.BlockSpec((d, d), lambda i: (0, 0)),
                  pl.BlockSpec((1, d), lambda i: (0, 0)),
                  pl.BlockSpec((1, d), lambda i: (0, 0))],
        out_specs=pl.BlockSpec((tm, d), lambda i: (i, 0)),
        out_shape=jax.ShapeDtypeStruct((t, d), F32),
        compiler_params=pltpu.CompilerParams(dimension_semantics=("parallel",)),
        name="merge_out",
    )(y_rw, y_m2, y_gla, proj, proj, proj, x2, mod, p["w_branch"], p["w_out"], p["post_g"],
      p["post_b"])


def _pad_cols(a, n):
    return jnp.pad(a, ((0, 0), (0, n - a.shape[1])))


def _pad_rows_at(a, start, n):
    return jnp.pad(a, ((start, n - start - a.shape[0]), (0, 0)))


def _layer_params(l, w_in, rw_mu, rw_w0, rw_w_up, rw_a0, rw_a_up, rw_k_k, rw_k_a, rw_r_k,
                  rw_ln_g, rw_ln_b, m2_conv_w, m2_conv_b, m2_dt_bias, m2_a_log, m2_d_skip,
                  m2_norm_w, gla_gk_up, gla_gk_b, gla_norm_w, w_branch, w_out, post_g, post_b):
    W = WIDTH
    rank2 = 2 * RW_RANK
    sizes = (3 * W + rank2, W, W + 2 * M2_GROUPS * M2_STATE, M2_HEADS, W,
             GLA_HEADS * GLA_DK, GLA_HEADS * GLA_DK, GLA_HEADS * GLA_DV, GLA_RANK, W,
             N_BRANCH * D_MODEL)
    offs = [0]
    for s in sizes:
        offs.append(offs[-1] + s)
    (o_rw, o_rwg, o_xbc, o_dt, o_z, o_q, o_k, o_v, o_gk, o_gg, o_merge, _) = offs
    wl = w_in[l]
    cols = lambda a, n: wl[:, a:a + n]
    segs = {
        "rw_r": cols(o_rw, W), "rw_k": cols(o_rw + W, W), "rw_v": cols(o_rw + 2 * W, W),
        "rw_gate": cols(o_rwg, W), "m2_x": cols(o_xbc, W), "m2_bc": cols(o_xbc + W, W),
        "m2_z": cols(o_z, W), "gla_qk": cols(o_q, W), "gla_v": cols(o_v, W),
        "gla_gate": cols(o_gg, W),
        "rw_wa": cols(o_rw + 3 * W, rank2), "m2_dt": _pad_cols(cols(o_dt, M2_HEADS), LANE),
        "gla_gk": _pad_cols(cols(o_gk, GLA_RANK), LANE),
    }
    w_cat = jnp.concatenate([cols(o_merge, N_BRANCH * D_MODEL)] + [segs[n] for n in SEG512]
                            + [segs[n] for n in SEG128], axis=1)
    w_cat = _pad_cols(w_cat, PROJ_WIDTH).astype(BF16)

    row = lambda a: a.reshape(1, -1)
    mu = rw_mu[l]
    head_of_ch = jnp.arange(W) // RW_DIM
    ones_bd = (head_of_ch[:, None] == head_of_ch[None, :]).astype(BF16)
    expand = (jnp.arange(LANE)[:, None] == (jnp.arange(W) // M2_DIM)[None, :]).astype(BF16)
    cw = m2_conv_w[l]
    cb = m2_conv_b[l]
    return {
        "w_in": w_cat,
        "mu_r": row(mu[:W]), "mu_k": row(mu[W:2 * W]), "mu_v": row(mu[2 * W:3 * W]),
        "mu_wa": row(mu[3 * W:]),
        "w0": row(rw_w0[l]), "w_up": _pad_rows_at(rw_w_up[l], 0, LANE),
        "a0": row(rw_a0[l]), "a_up": _pad_rows_at(rw_a_up[l], RW_RANK, LANE),
        "k_k": row(rw_k_k[l]), "k_a": row(rw_k_a[l]), "r_k": row(rw_r_k[l]),
        "ln_g": row(rw_ln_g[l]), "ln_b": row(rw_ln_b[l]), "ones_bd": ones_bd,
        "cw_x": cw[:, :W], "cw_bc": cw[:, W:], "cb_x": row(cb[:W]), "cb_bc": row(cb[W:]),
        "dt_bias": _pad_cols(row(m2_dt_bias[l]), LANE), "a_log": _pad_cols(row(m2_a_log[l]), LANE),
        "d_skip": row(jnp.repeat(m2_d_skip[l], M2_DIM)), "m2_norm_w": row(m2_norm_w[l]),
        "expand": expand,
        "gk_up": _pad_rows_at(gla_gk_up[l], 0, LANE), "gk_b": row(gla_gk_b[l]),
        "gla_norm_w": row(jnp.tile(gla_norm_w[l], GLA_HEADS)),
        "w_branch": w_branch[l].astype(BF16), "w_out": w_out[l].astype(BF16),
        "post_g": row(post_g[l]), "post_b": row(post_b[l]),
    }


def kernel(x, c, ada_w, ada_b, w_in, rw_mu, rw_w0, rw_w_up, rw_a0, rw_a_up, rw_k_k, rw_k_a, rw_r_k, rw_ln_g, rw_ln_b, m2_conv_w, m2_conv_b, m2_dt_bias, m2_a_log, m2_d_skip, m2_norm_w, gla_gk_up, gla_gk_b, gla_norm_w, w_branch, w_out, post_g, post_b):
    bsz, seq, d = x.shape
    depth = ada_w.shape[0]
    alpha = (2.0 * depth) ** 0.25
    mod_all = _adaln(c, ada_w, ada_b).reshape(depth, bsz, 3, d)
    x2 = x.reshape(bsz * seq, d)
    for l in range(depth):
        p = _layer_params(l, w_in, rw_mu, rw_w0, rw_w_up, rw_a0, rw_a_up, rw_k_k, rw_k_a,
                          rw_r_k, rw_ln_g, rw_ln_b, m2_conv_w, m2_conv_b, m2_dt_bias, m2_a_log,
                          m2_d_skip, m2_norm_w, gla_gk_up, gla_gk_b, gla_norm_w, w_branch, w_out,
                          post_g, post_b)
        mod = mod_all[l]
        proj = _inproj(x2, mod, p["w_in"], seq)
        y_rw = _rwkv(proj, p, bsz, seq)
        y_m2 = _mamba(proj, p, bsz, seq)
        y_gla = _gla(proj, p, bsz, seq)
        x2 = _merge(y_rw, y_m2, y_gla, proj, x2, mod, p, seq, alpha)
    return x2.reshape(bsz, seq, d)
```

```python
import functools

import jax
import jax.numpy as jnp
from jax import lax
from jax.experimental import pallas as pl
from jax.experimental.pallas import tpu as pltpu

F32 = jnp.float32
BF16 = jnp.bfloat16
HIGHEST = lax.Precision.HIGHEST

D_MODEL = 1024
WIDTH = 512
LN_EPS = 1e-5
RW_HEADS, RW_DIM, RW_RANK = 8, 64, 64
RW_GN_EPS = 64e-5
RW_CHUNK = 64
RW_SUBCHUNKS = 4
RW_BLOCK = 2 * RW_CHUNK
M2_HEADS, M2_DIM, M2_GROUPS, M2_STATE, M2_CONV = 8, 64, 2, 128, 4
M2_CHUNK = 128
M2_EPS = 1e-5
GLA_HEADS, GLA_DK, GLA_DV, GLA_RANK = 4, 64, 128, 16
GLA_TAU = 16.0
GLA_CHUNK = 64
GLA_SUBCHUNKS = 2
GLA_EPS = 1e-5
N_BRANCH = 3
MIX_TILE = 512

LANE = 128
SUBLANE = 8
MXU_DIM = 256

SEG512 = ("rw_r", "rw_k", "rw_v", "rw_gate", "m2_x", "m2_bc", "m2_z", "gla_qk", "gla_v", "gla_gate")
SEG128 = ("rw_wa", "m2_dt", "gla_gk")
COL512_BASE = N_BRANCH * D_MODEL
COL128_BASE = COL512_BASE + WIDTH * len(SEG512)
PROJ_WIDTH = -(-(COL128_BASE + LANE * len(SEG128)) // WIDTH) * WIDTH
PROJ_COL_TILES = 4
PROJ_TILE_N = PROJ_WIDTH // PROJ_COL_TILES
assert PROJ_TILE_N * PROJ_COL_TILES == PROJ_WIDTH and PROJ_TILE_N % LANE == 0

P_RW = 1
P_M2 = 1
P_GLA = 1
P_LOWRANK = 3
HEAD_SUM_PIECES = 1

NN = ((1,), (0,))
NT = ((1,), (1,))
TN = ((0,), (0,))


def _blk512(name):
    return (COL512_BASE + WIDTH * SEG512.index(name)) // WIDTH


def _blk128(name):
    return (COL128_BASE + LANE * SEG128.index(name)) // LANE


def _bdot(a, b):
    return jnp.dot(a.astype(BF16), b.astype(BF16), preferred_element_type=F32)


def _dg(a, b, dims):
    return lax.dot_general(a, b, (dims, ((), ())), preferred_element_type=F32)


def _split2(x):
    hi = x.astype(BF16)
    return hi, (x - hi.astype(F32)).astype(BF16)


def _dot(a, b, dims=NN, passes=1):
    if passes == 1:
        return _dg(a.astype(BF16), b.astype(BF16), dims)
    a_hi, a_lo = _split2(a)
    b_hi, b_lo = _split2(b)
    return _dg(a_hi, b_hi, dims) + _dg(a_lo, b_hi, dims) + _dg(a_hi, b_lo, dims)


def _dot_exact_rhs(a, b_bf16, pieces=2):
    out = None
    for _ in range(pieces):
        piece = a.astype(BF16)
        a = a - piece.astype(F32)
        term = _dg(piece, b_bf16, NN)
        out = term if out is None else out + term
    return out


def _chunk_cumsum(x, chunk):
    pos = lax.broadcasted_iota(jnp.int32, x.shape, 0) & (chunk - 1)
    step = 1
    while step < chunk:
        x = x + jnp.where(pos >= step, pltpu.roll(x, step, 0), 0.0)
        step *= 2
    return x


def _sigmoid(x):
    return jax.nn.sigmoid(x)


def _silu(x):
    return x * jax.nn.sigmoid(x)


def _softplus(x):
    return jnp.maximum(x, 0.0) + jnp.log1p(jnp.exp(-jnp.abs(x)))


def _tri(n):
    return (lax.broadcasted_iota(jnp.int32, (n, n), 0)
            >= lax.broadcasted_iota(jnp.int32, (n, n), 1))


def _shift_rows(x, prev, j):
    ext = jnp.concatenate([prev, x], axis=0)
    return pltpu.roll(ext, j, 0)[SUBLANE:, :]


def _emit_interleaved(stages, fillers):
    n_s, n_f = len(stages), len(fillers)
    after = [((k + 1) * n_s) // (n_f + 1) for k in range(n_f)]
    k = 0
    for i, stage in enumerate(stages):
        stage()
        while k < n_f and after[k] <= i + 1:
            fillers[k]()
            k += 1
    for filler in fillers[k:]:
        filler()


def _adaln_kernel(c_ref, w_ref, b_ref, o_ref):
    o_ref[...] = lax.dot_general(_silu(c_ref[...]), w_ref[...], (NN, ((), ())), precision=HIGHEST,
                                 preferred_element_type=F32) + b_ref[...]


def _adaln(c, ada_w, ada_b):
    depth, d, n = ada_w.shape
    bsz = c.shape[0]
    tn = 768
    return pl.pallas_call(
        _adaln_kernel,
        grid=(depth, n // tn),
        in_specs=[pl.BlockSpec((bsz, d), lambda l, j: (0, 0)),
                  pl.BlockSpec((None, d, tn), lambda l, j: (l, 0, j)),
                  pl.BlockSpec((None, 1, tn), lambda l, j: (l, 0, j))],
        out_specs=pl.BlockSpec((None, bsz, tn), lambda l, j: (l, 0, j)),
        out_shape=jax.ShapeDtypeStruct((depth, bsz, n), F32),
        name="adaln_mod",
    )(c, ada_w, ada_b.reshape(depth, 1, n))


def _inproj_kernel(x_ref, mod_ref, w_ref, o_ref, h_scr):
    @pl.when(pl.program_id(1) == 0)
    def _():
        x = x_ref[...]
        mu = jnp.mean(x, axis=-1, keepdims=True)
        xc = x - mu
        var = jnp.mean(xc * xc, axis=-1, keepdims=True)
        h = xc * lax.rsqrt(var + LN_EPS) * (1.0 + mod_ref[1:2, :]) + mod_ref[0:1, :]
        h_scr[...] = h.astype(BF16)

    o_ref[...] = jnp.dot(h_scr[...], w_ref[...], preferred_element_type=F32)


def _inproj(x2, mod, w_bf16, seq):
    t, d = x2.shape
    n = w_bf16.shape[1]
    tm = min(1024, seq)
    tiles_per_seq = seq // tm
    return pl.pallas_call(
        _inproj_kernel,
        grid=(t // tm, n // PROJ_TILE_N),
        in_specs=[pl.BlockSpec((tm, d), lambda i, j: (i, 0)),
                  pl.BlockSpec((None, 3, d), lambda i, j: (i // tiles_per_seq, 0, 0)),
                  pl.BlockSpec((d, PROJ_TILE_N), lambda i, j: (0, j))],
        out_specs=pl.BlockSpec((tm, PROJ_TILE_N), lambda i, j: (i, j)),
        out_shape=jax.ShapeDtypeStruct((t, n), F32),
        scratch_shapes=[pltpu.VMEM((tm, d), BF16)],
        compiler_params=pltpu.CompilerParams(dimension_semantics=("parallel", "arbitrary")),
        name="inproj",
    )(x2, mod, w_bf16)


def _rwkv_plan(tb, r_ref, k_ref, v_ref, g_ref, wa_ref, mu_r, mu_k, mu_v, mu_wa, w0, w_up, a0, a_up,
               k_k, k_a, r_k, ln_g, ln_b, ones_bd, o_ref, car_r, car_k, car_v, car_wa, st):
    L, N, RB = RW_CHUNK, RW_DIM, RW_BLOCK
    nchunk, nblock = tb // L, tb // RB
    blocks = [dict() for _ in range(nblock)]
    ys = [[None] * (RW_HEADS // 2) for _ in range(nchunk)]

    def head_sum(t):
        parts = [_dot_exact_rhs(t[:, c:c + MXU_DIM], ones_bd[...], HEAD_SUM_PIECES)
                 for c in range(0, WIDTH, MXU_DIM)]
        return jnp.concatenate(parts, axis=1)

    def prepare(bi):
        d = blocks[bi]
        rows = slice(bi * RB, (bi + 1) * RB)

        def lerp(x_ref, car, mu):
            x = x_ref[rows, :]
            prev = car[...] if bi == 0 else x_ref[bi * RB - SUBLANE:bi * RB, :]
            if bi == nblock - 1:
                car[...] = x[RB - SUBLANE:, :]
            return x + mu[...] * (_shift_rows(x, prev, 1) - x)

        def p_rk():
            d["r"] = lerp(r_ref, car_r, mu_r)
            d["k"] = lerp(k_ref, car_k, mu_k)

        def p_v():
            d["v"] = lerp(v_ref, car_v, mu_v)
            d["wa"] = lerp(wa_ref, car_wa, mu_wa)

        def p_decay():
            w = w0[...] + _dot(jnp.tanh(d["wa"]), w_up[...], NN, P_LOWRANK)
            d["lw"] = -0.6065306597126334 * _sigmoid(w)
            d["a"] = _sigmoid(a0[...] + _dot(d["wa"], a_up[...], NN, P_LOWRANK))

        def p_kk():
            kk = d["k"] * k_k[...]
            d["kk"] = kk * lax.rsqrt(jnp.maximum(head_sum(kk * kk), 1e-24))

        def p_k2():
            d["k2"] = d["k"] * (1.0 + (d["a"] - 1.0) * k_a[...])
            d["bonus"] = head_sum(d["r"] * d["k2"] * r_k[...]) * d["v"]
            d["b"] = d["kk"] * d["a"]

        def p_cumsum():
            d["g"] = _chunk_cumsum(d["lw"], L)

        def p_scale():
            g = d["g"]
            eng = jnp.exp(-g)
            d["at"] = -d["kk"] * jnp.exp(g - d["lw"])
            d["bt"] = d["b"] * eng
            d["kt"] = d["k2"] * eng
            d["rt"] = d["r"] * jnp.exp(g)

        return [p_rk, p_v, p_decay, p_kk, p_k2, p_cumsum, p_scale]

    P2 = 2 * L
    npair = RW_HEADS // 2
    lane_lo = lax.broadcasted_iota(jnp.int32, (L, P2), 1) < N

    def hat(x):
        return jnp.concatenate([jnp.where(lane_lo, x, 0.0), jnp.where(lane_lo, 0.0, x)], axis=0)

    ri = lax.broadcasted_iota(jnp.int32, (2 * P2, 2 * P2), 0)
    ci = lax.broadcasted_iota(jnp.int32, (2 * P2, 2 * P2), 1)
    same_head = ((ri & (P2 - 1)) >= L) == ((ci & (P2 - 1)) >= L)
    tpos = ri & (L - 1)
    spos = ci & (L - 1)
    mask4 = same_head & (jnp.where(ri < P2, tpos, tpos + 1) > spos)
    eye = (lax.broadcasted_iota(jnp.int32, (P2, P2), 0)
           == lax.broadcasted_iota(jnp.int32, (P2, P2), 1)).astype(F32)

    def group(chunks):
        probs = [dict(i=i, j=j) for i in chunks for j in range(npair)]

        def load(i):
            d = blocks[(i * L) // RB]
            rows = slice((i * L) % RB, (i * L) % RB + L)
            gc = d["g"][rows]
            gl = gc[L - 1:L, :]
            egl = jnp.exp(gl - gc)
            dec = jnp.exp(gl)
            bh = d["b"][rows] * egl
            kh = d["k2"][rows] * egl
            for q in probs:
                if q["i"] != i:
                    continue
                ls = slice(q["j"] * P2, (q["j"] + 1) * P2)
                q.update(dec=dec[:, ls], a=hat(d["at"][rows, ls]), r=hat(d["rt"][rows, ls]),
                         b=hat(d["bt"][rows, ls]), k=hat(d["kt"][rows, ls]),
                         v=hat(d["v"][rows, ls]), bh=hat(bh[:, ls]), kh=hat(kh[:, ls]))

        def interactions(q):
            m = _dot(jnp.concatenate([q["a"], q["r"]], axis=0),
                     jnp.concatenate([q["b"], q["k"]], axis=0), NT, P_RW)
            m = jnp.where(mask4, m, 0.0)
            q["a_ab"], q["a_ak"] = m[:P2, :P2], m[:P2, P2:]
            q["a_rb"], q["a_rk"] = m[P2:, :P2], m[P2:, P2:]
            q["p"] = q["a_ab"]
            q["t"] = eye + q["a_ab"]

        def square(q):
            q["p"] = _dot(q["p"], q["p"], NN, P_RW)

        def double(q):
            tp = _dot(q["p"], jnp.concatenate([q["t"], q["p"]], axis=1), NN, P_RW)
            q["t"], q["p"] = q["t"] + tp[:, :P2], tp[:, P2:]

        def inverse_done(q):
            q["t"] = q["t"] + _dot(q["p"], q["t"], NN, P_RW)

        def av(q):
            q["av"] = _dot(q["a_ak"], q["v"], NN, P_RW)

        def tu(q):
            q["tu"] = _dot(q["t"], jnp.concatenate([q["a"], q["av"]], axis=1), NN, P_RW)

        def ru(q):
            lhs = jnp.concatenate([q["a_rb"], q["a_rk"]], axis=1)
            rhs = jnp.concatenate(
                [q["tu"], jnp.concatenate([jnp.zeros_like(q["v"]), q["v"]], axis=1)], axis=0)
            out = _dot(lhs, rhs, NN, P_RW)
            q["rm"] = q["r"] + out[:, :P2]
            q["y0"] = out[:, P2:]

        def maps(q):
            q["mt"] = eye * q["dec"] + _dot(q["tu"][:, :P2], q["bh"], TN, P_RW)
            q["cc"] = _dot(jnp.concatenate([q["tu"][:, P2:], q["v"]], axis=0),
                           jnp.concatenate([q["bh"], q["kh"]], axis=0), TN, P_RW)

        def recur(q):
            s0 = st[q["j"]]
            yh = _dot(q["rm"], s0, NT, P_RW) + q["y0"]
            ys[q["i"]][q["j"]] = yh[:L] + yh[L:]
            st[q["j"]] = _dot(s0, q["mt"], NN, P_RW) + q["cc"]

        pieces = [functools.partial(load, i) for i in chunks]
        for stage in (interactions, square, double, double, double, double, inverse_done,
                      av, tu, ru, maps, recur):
            pieces += [functools.partial(stage, q) for q in probs]
        return pieces

    def close(bi):
        d = blocks[bi]
        rows = slice(bi * RB, (bi + 1) * RB)

        def c_mean():
            y = jnp.concatenate([jnp.concatenate(ys[i], axis=1)
                                 for i in range(bi * RB // L, (bi + 1) * RB // L)], axis=0)
            d["yc"] = y - head_sum(y) * (1.0 / N)

        def c_var():
            d["var"] = head_sum(d["yc"] * d["yc"]) * (1.0 / N)

        def c_out():
            yn = d["yc"] * lax.rsqrt(d["var"] + RW_GN_EPS) * ln_g[...] + ln_b[...]
            o_ref[rows, :] = ((yn + d["bonus"]) * _silu(g_ref[rows, :])).astype(o_ref.dtype)

        return [c_mean, c_var, c_out]

    groups = [group(range(c0, c0 + RW_SUBCHUNKS)) for c0 in range(0, nchunk, RW_SUBCHUNKS)]
    return ([prepare(bi) for bi in range(nblock)], groups, [close(bi) for bi in range(nblock)])


def _mamba_pieces(tb, x_ref, bc_ref, z_ref, dt_ref, cw_x, cw_bc, cb_x, cb_bc, dt_bias, a_log,
                  d_skip, norm_w, expand, o_ref, car_x, car_bc, st, y_scr):
    L = M2_CHUNK
    nchunk = tb // L
    gw = WIDTH // M2_GROUPS
    hpg = M2_HEADS // M2_GROUPS
    tri_incl = _tri(L)

    def chunk(c):
        d = {}
        rows = slice(c * L, (c + 1) * L)

        def conv(ref, car, w, bias):
            x = ref[rows, :]
            prev = car[...] if c == 0 else ref[c * L - SUBLANE:c * L, :]
            y = x * w[M2_CONV - 1:M2_CONV, :] + bias[...]
            for j in range(1, M2_CONV):
                y = y + _shift_rows(x, prev, j) * w[M2_CONV - 1 - j:M2_CONV - j, :]
            if c == nchunk - 1:
                car[...] = x[L - SUBLANE:, :]
            return _silu(y)

        def m_conv_x():
            d["xs"] = conv(x_ref, car_x, cw_x, cb_x)

        def m_conv_bc():
            bcc = conv(bc_ref, car_bc, cw_bc, cb_bc)
            d["bm"] = [bcc[:, g * M2_STATE:(g + 1) * M2_STATE] for g in range(M2_GROUPS)]
            d["cm"] = [bcc[:, gw + g * M2_STATE:gw + (g + 1) * M2_STATE]
                       for g in range(M2_GROUPS)]

        def m_decay():
            ex = expand[...]
            dt = _softplus(dt_ref[rows, :] + dt_bias[...])
            a_c = _chunk_cumsum(dt * (-jnp.exp(a_log[...])), L)
            d["a_c"], d["a_t"] = a_c, a_c.T
            a_x = _dot_exact_rhs(a_c, ex, 3)
            d["xdt"] = d["xs"] * _dot_exact_rhs(dt, ex, 3)
            a_last = a_x[L - 1:L, :]
            d["xdt_end"] = d["xdt"] * jnp.exp(a_last - a_x)
            d["e_ax"] = jnp.exp(a_x)
            d["e_last"] = jnp.exp(a_last)

        def m_cb():
            d["cb"] = [_dot(d["cm"][g], d["bm"][g], NT, P_M2) for g in range(M2_GROUPS)]

        def m_head(h):
            diff = d["a_c"][:, h:h + 1] - d["a_t"][h:h + 1, :]
            dec = jnp.exp(jnp.where(tri_incl, diff, -jnp.inf))
            d["yd%d" % h] = _dot(d["cb"][h // hpg] * dec,
                                 d["xdt"][:, h * M2_DIM:(h + 1) * M2_DIM], NN, P_M2)

        def m_state(g):
            gs = slice(g * gw, (g + 1) * gw)
            kv = _dot(d["bm"][g], d["xdt_end"][:, gs], TN, P_M2)
            s0 = st[g]
            y_off = _dot(d["cm"][g], s0, NN, P_M2) * d["e_ax"][:, gs]
            st[g] = s0 * d["e_last"][:, gs] + kv
            for hh in range(hpg):
                h = g * hpg + hh
                y_scr[rows, h * M2_DIM:(h + 1) * M2_DIM] = (
                    d["yd%d" % h] + y_off[:, hh * M2_DIM:(hh + 1) * M2_DIM])

        def m_out():
            y = (y_scr[rows, :] + d["xs"] * d_skip[...]) * _silu(z_ref[rows, :])
            outs = []
            for g in range(M2_GROUPS):
                yg = y[:, g * gw:(g + 1) * gw]
                ms = jnp.mean(yg * yg, axis=-1, keepdims=True)
                outs.append(yg * lax.rsqrt(ms + M2_EPS))
            o_ref[rows, :] = (jnp.concatenate(outs, axis=-1) * norm_w[...]).astype(o_ref.dtype)

        return ([m_conv_x, m_conv_bc, m_decay, m_cb]
                + [functools.partial(m_head, h) for h in range(M2_HEADS)]
                + [functools.partial(m_state, g) for g in range(M2_GROUPS)] + [m_out])

    return [chunk(c) for c in range(nchunk)]


def _gla_pieces(tb, qk_ref, v_ref, g_ref, gk_ref, gk_up, gk_b, norm_w, o_ref, st):
    L = GLA_CHUNK
    span = L * GLA_SUBCHUNKS
    kd = GLA_HEADS * GLA_DK
    tri_incl = _tri(L)

    def block(i):
        d = {}
        rows = slice(i * span, (i + 1) * span)
        probs = [dict(sub=sub, h=h) for sub in range(GLA_SUBCHUNKS) for h in range(GLA_HEADS)]

        def g_gates():
            la = -_softplus(-(_dot(gk_ref[rows, :], gk_up[...], NN, P_LOWRANK) + gk_b[...]))
            g = _chunk_cumsum(la * (1.0 / GLA_TAU), L)
            qk = qk_ref[rows, :]
            d["g"], d["kraw"] = g, qk[:, kd:]
            d["qg"] = qk[:, :kd] * (GLA_DK ** -0.5) * jnp.exp(g)
            d["kg"] = d["kraw"] * jnp.exp(-g)

        def g_split():
            vv = v_ref[rows, :]
            for sub in range(GLA_SUBCHUNKS):
                rs = slice(sub * L, (sub + 1) * L)
                gc = d["g"][rs]
                gl = gc[L - 1:L, :]
                kh = d["kraw"][rs] * jnp.exp(gl - gc)
                dec = jnp.exp(gl)
                for q in probs:
                    if q["sub"] != sub:
                        continue
                    ks = slice(q["h"] * GLA_DK, (q["h"] + 1) * GLA_DK)
                    vs = slice(q["h"] * GLA_DV, (q["h"] + 1) * GLA_DV)
                    q.update(qg=d["qg"][rs, ks], kg=d["kg"][rs, ks], v=vv[rs, vs], kh=kh[:, ks],
                             dec=dec[:, ks])

        def g_att(sub):
            for q in probs:
                if q["sub"] == sub:
                    q["att"] = jnp.where(tri_incl, _dot(q["qg"], q["kg"], NT, P_GLA), 0.0)

        def g_av(sub):
            for q in probs:
                if q["sub"] == sub:
                    q["o"] = _dot(q["att"], q["v"], NN, P_GLA)
                    q["kv"] = _dot(q["v"], q["kh"], TN, P_GLA)

        def g_recur(sub):
            for q in probs:
                if q["sub"] == sub:
                    s0 = st[q["h"]]
                    oh = q["o"] + _dot(q["qg"], s0, NT, P_GLA)
                    st[q["h"]] = s0 * q["dec"] + q["kv"]
                    ms = jnp.mean(oh * oh, axis=-1, keepdims=True)
                    q["on"] = oh * lax.rsqrt(ms + GLA_EPS)

        def g_out():
            o = jnp.concatenate(
                [jnp.concatenate([q["on"] for q in probs if q["sub"] == sub], axis=1)
                 for sub in range(GLA_SUBCHUNKS)], axis=0)
            o_ref[rows, :] = (o * norm_w[...] * _silu(g_ref[rows, :])).astype(o_ref.dtype)

        subs = range(GLA_SUBCHUNKS)
        return ([g_gates, g_split] + [functools.partial(g_att, s) for s in subs]
                + [functools.partial(g_av, s) for s in subs]
                + [functools.partial(g_recur, s) for s in subs] + [g_out])

    return [block(i) for i in range(tb // span)]


def _mixers_kernel(r_ref, k_ref, v_ref, rg_ref, wa_ref, mx_ref, mbc_ref, mz_ref, mdt_ref,
                   gqk_ref, gv_ref, gg_ref, ggk_ref,
                   mu_r, mu_k, mu_v, mu_wa, w0, w_up, a0, a_up, k_k, k_a, r_k, ln_g, ln_b, ones_bd,
                   cw_x, cw_bc, cb_x, cb_bc, dt_bias, a_log, d_skip, m2_norm_w, expand,
                   gk_up, gk_b, gla_norm_w,
                   o_rw, o_m2, o_gla,
                   car_r, car_k, car_v, car_wa, st_rw, car_x, car_bc, st_m2, y_m2, st_gla):
    tb = r_ref.shape[0]

    @pl.when(pl.program_id(1) == 0)
    def _():
        for ref in (car_r, car_k, car_v, car_wa, st_rw, car_x, car_bc, st_m2, st_gla):
            ref[...] = jnp.zeros_like(ref)

    prepare, groups, close = _rwkv_plan(
        tb, r_ref, k_ref, v_ref, rg_ref, wa_ref, mu_r, mu_k, mu_v, mu_wa, w0, w_up, a0, a_up,
        k_k, k_a, r_k, ln_g, ln_b, ones_bd, o_rw, car_r, car_k, car_v, car_wa, st_rw)
    m2 = _mamba_pieces(tb, mx_ref, mbc_ref, mz_ref, mdt_ref, cw_x, cw_bc, cb_x, cb_bc, dt_bias,
                       a_log, d_skip, m2_norm_w, expand, o_m2, car_x, car_bc, st_m2, y_m2)
    gla = _gla_pieces(tb, gqk_ref, gv_ref, gg_ref, ggk_ref, gk_up, gk_b, gla_norm_w, o_gla,
                      st_gla)

    ngroup = len(groups)
    bpg = len(prepare) // ngroup
    share = lambda items, n: items[n * len(items) // ngroup:(n + 1) * len(items) // ngroup]
    flat = lambda lists: [piece for pieces in lists for piece in pieces]
    for piece in flat(prepare[:bpg]):
        piece()
    for n in range(ngroup):
        fillers = flat(prepare[(n + 1) * bpg:(n + 2) * bpg])
        if n > 0:
            fillers += flat(close[(n - 1) * bpg:n * bpg])
        side = [flat(share(m2, n)), flat(share(gla, n))]
        fillers += [piece for k in range(max(map(len, side))) for s in side for piece in s[k:k + 1]]
        _emit_interleaved(groups[n], fillers)
    for piece in flat(close[(ngroup - 1) * bpg:]):
        piece()


def _mixers(proj, p, bsz, seq):
    tb = min(MIX_TILE, seq)
    assert tb % (RW_SUBCHUNKS * RW_CHUNK) == 0 and tb % M2_CHUNK == 0
    spt = seq // tb
    row = lambda b, s: b * spt + s
    seg = lambda name: pl.BlockSpec((tb, WIDTH), lambda b, s, j=_blk512(name): (row(b, s), j))
    low = lambda name: pl.BlockSpec((tb, LANE), lambda b, s, j=_blk128(name): (row(b, s), j))
    vec = lambda n: pl.BlockSpec((1, n), lambda b, s: (0, 0))
    mat = lambda m, n: pl.BlockSpec((m, n), lambda b, s: (0, 0))
    out = pl.BlockSpec((tb, WIDTH), lambda b, s: (row(b, s), 0))
    kd = GLA_HEADS * GLA_DK
    carry = lambda n: pltpu.VMEM((SUBLANE, n), F32)
    y_shape = jax.ShapeDtypeStruct((bsz * seq, WIDTH), BF16)
    return pl.pallas_call(
        _mixers_kernel,
        grid=(bsz, spt),
        in_specs=[seg("rw_r"), seg("rw_k"), seg("rw_v"), seg("rw_gate"), low("rw_wa"),
                  seg("m2_x"), seg("m2_bc"), seg("m2_z"), low("m2_dt"),
                  seg("gla_qk"), seg("gla_v"), seg("gla_gate"), low("gla_gk"),
                  vec(WIDTH), vec(WIDTH), vec(WIDTH), vec(LANE), vec(WIDTH), mat(LANE, WIDTH),
                  vec(WIDTH), mat(LANE, WIDTH), vec(WIDTH), vec(WIDTH), vec(WIDTH), vec(WIDTH),
                  vec(WIDTH), mat(MXU_DIM, MXU_DIM),
                  mat(M2_CONV, WIDTH), mat(M2_CONV, WIDTH), vec(WIDTH), vec(WIDTH),
                  vec(LANE), vec(LANE), vec(WIDTH), vec(WIDTH), mat(LANE, WIDTH),
                  mat(LANE, kd), vec(kd), vec(WIDTH)],
        out_specs=[out, out, out],
        out_shape=[y_shape, y_shape, y_shape],
        scratch_shapes=[carry(WIDTH), carry(WIDTH), carry(WIDTH), carry(LANE),
                        pltpu.VMEM((RW_HEADS // 2, 2 * RW_DIM, 2 * RW_DIM), F32),
                        carry(WIDTH), carry(WIDTH),
                        pltpu.VMEM((M2_GROUPS, M2_STATE, WIDTH // M2_GROUPS), F32),
                        pltpu.VMEM((tb, WIDTH), F32),
                        pltpu.VMEM((GLA_HEADS, GLA_DV, GLA_DK), F32)],
        compiler_params=pltpu.CompilerParams(dimension_semantics=("arbitrary", "arbitrary")),
        name="mixers",
    )(*([proj] * 13),
      p["mu_r"], p["mu_k"], p["mu_v"], p["mu_wa"], p["w0"], p["w_up"], p["a0"], p["a_up"],
      p["k_k"], p["k_a"], p["r_k"], p["ln_g"], p["ln_b"], p["ones_bd"],
      p["cw_x"], p["cw_bc"], p["cb_x"], p["cb_bc"], p["dt_bias"], p["a_log"], p["d_skip"],
      p["m2_norm_w"], p["expand"],
      p["gk_up"], p["gk_b"], p["gla_norm_w"])


def _merge_kernel(alpha, y_rw, y_m2, y_gla, lg0, lg1, lg2, x_ref, mod_ref, wb, w_out, pg, pb,
                  o_ref):
    merged = (_sigmoid(lg0[...]) * _bdot(y_rw[...], wb[0])
              + _sigmoid(lg1[...]) * _bdot(y_m2[...], wb[1])
              + _sigmoid(lg2[...]) * _bdot(y_gla[...], wb[2]))
    y = _bdot(merged, w_out[...])
    res = alpha * x_ref[...] + (1.0 + mod_ref[2:3, :]) * y
    mu = jnp.mean(res, axis=-1, keepdims=True)
    rc = res - mu
    var = jnp.mean(rc * rc, axis=-1, keepdims=True)
    o_ref[...] = rc * lax.rsqrt(var + LN_EPS) * pg[...] + pb[...]


def _merge(y_rw, y_m2, y_gla, proj, x2, mod, p, seq, alpha):
    t, d = x2.shape
    tm = min(512, seq)
    tiles_per_seq = seq // tm
    ytile = lambda: pl.BlockSpec((tm, WIDTH), lambda i: (i, 0))
    logit = lambda j: pl.BlockSpec((tm, d), lambda i, j=j: (i, j))
    return pl.pallas_call(
        functools.partial(_merge_kernel, alpha),
        grid=(t // tm,),
        in_specs=[ytile(), ytile(), ytile(), logit(0), logit(1), logit(2),
                  pl.BlockSpec((tm, d), lambda i: (i, 0)),
                  pl.BlockSpec((None, 3, d), lambda i: (i // tiles_per_seq, 0, 0)),
                  pl.BlockSpec((N_BRANCH, WIDTH, d), lambda i: (0, 0, 0)),
                  pl.BlockSpec((d, d), lambda i: (0, 0)),
                  pl.BlockSpec((1, d), lambda i: (0, 0)),
                  pl.BlockSpec((1, d), lambda i: (0, 0))],
        out_specs=pl.BlockSpec((tm, d), lambda i: (i, 0)),
        out_shape=jax.ShapeDtypeStruct((t, d), F32),
        compiler_params=pltpu.CompilerParams(dimension_semantics=("parallel",)),
        name="merge_out",
    )(y_rw, y_m2, y_gla, proj, proj, proj, x2, mod, p["w_branch"], p["w_out"], p["post_g"],
      p["post_b"])


def _pad_cols(a, n):
    return jnp.pad(a, ((0, 0), (0, n - a.shape[1])))


def _pad_rows_at(a, start, n):
    return jnp.pad(a, ((start, n - start - a.shape[0]), (0, 0)))


def _layer_params(l, w_in, rw_mu, rw_w0, rw_w_up, rw_a0, rw_a_up, rw_k_k, rw_k_a, rw_r_k,
                  rw_ln_g, rw_ln_b, m2_conv_w, m2_conv_b, m2_dt_bias, m2_a_log, m2_d_skip,
                  m2_norm_w, gla_gk_up, gla_gk_b, gla_norm_w, w_branch, w_out, post_g, post_b):
    W = WIDTH
    rank2 = 2 * RW_RANK
    sizes = (3 * W + rank2, W, W + 2 * M2_GROUPS * M2_STATE, M2_HEADS, W,
             GLA_HEADS * GLA_DK, GLA_HEADS * GLA_DK, GLA_HEADS * GLA_DV, GLA_RANK, W,
             N_BRANCH * D_MODEL)
    offs = [0]
    for s in sizes:
        offs.append(offs[-1] + s)
    (o_rw, o_rwg, o_xbc, o_dt, o_z, o_q, o_k, o_v, o_gk, o_gg, o_merge, _) = offs
    wl = w_in[l]
    cols = lambda a, n: wl[:, a:a + n]
    segs = {
        "rw_r": cols(o_rw, W), "rw_k": cols(o_rw + W, W), "rw_v": cols(o_rw + 2 * W, W),
        "rw_gate": cols(o_rwg, W), "m2_x": cols(o_xbc, W), "m2_bc": cols(o_xbc + W, W),
        "m2_z": cols(o_z, W), "gla_qk": cols(o_q, W), "gla_v": cols(o_v, W),
        "gla_gate": cols(o_gg, W),
        "rw_wa": cols(o_rw + 3 * W, rank2), "m2_dt": _pad_cols(cols(o_dt, M2_HEADS), LANE),
        "gla_gk": _pad_cols(cols(o_gk, GLA_RANK), LANE),
    }
    w_cat = jnp.concatenate([cols(o_merge, N_BRANCH * D_MODEL)] + [segs[n] for n in SEG512]
                            + [segs[n] for n in SEG128], axis=1)
    w_cat = _pad_cols(w_cat, PROJ_WIDTH).astype(BF16)

    row = lambda a: a.reshape(1, -1)
    mu = rw_mu[l]
    head_of_ch = jnp.arange(MXU_DIM) // RW_DIM
    ones_bd = (head_of_ch[:, None] == head_of_ch[None, :]).astype(BF16)
    expand = (jnp.arange(LANE)[:, None] == (jnp.arange(W) // M2_DIM)[None, :]).astype(BF16)
    cw = m2_conv_w[l]
    cb = m2_conv_b[l]
    return {
        "w_in": w_cat,
        "mu_r": row(mu[:W]), "mu_k": row(mu[W:2 * W]), "mu_v": row(mu[2 * W:3 * W]),
        "mu_wa": row(mu[3 * W:]),
        "w0": row(rw_w0[l]), "w_up": _pad_rows_at(rw_w_up[l], 0, LANE),
        "a0": row(rw_a0[l]), "a_up": _pad_rows_at(rw_a_up[l], RW_RANK, LANE),
        "k_k": row(rw_k_k[l]), "k_a": row(rw_k_a[l]), "r_k": row(rw_r_k[l]),
        "ln_g": row(rw_ln_g[l]), "ln_b": row(rw_ln_b[l]), "ones_bd": ones_bd,
        "cw_x": cw[:, :W], "cw_bc": cw[:, W:], "cb_x": row(cb[:W]), "cb_bc": row(cb[W:]),
        "dt_bias": _pad_cols(row(m2_dt_bias[l]), LANE), "a_log": _pad_cols(row(m2_a_log[l]), LANE),
        "d_skip": row(jnp.repeat(m2_d_skip[l], M2_DIM)), "m2_norm_w": row(m2_norm_w[l]),
        "expand": expand,
        "gk_up": _pad_rows_at(gla_gk_up[l], 0, LANE), "gk_b": row(gla_gk_b[l]),
        "gla_norm_w": row(jnp.tile(gla_norm_w[l], GLA_HEADS)),
        "w_branch": w_branch[l].astype(BF16), "w_out": w_out[l].astype(BF16),
        "post_g": row(post_g[l]), "post_b": row(post_b[l]),
    }


def kernel(x, c, ada_w, ada_b, w_in, rw_mu, rw_w0, rw_w_up, rw_a0, rw_a_up, rw_k_k, rw_k_a, rw_r_k, rw_ln_g, rw_ln_b, m2_conv_w, m2_conv_b, m2_dt_bias, m2_a_log, m2_d_skip, m2_norm_w, gla_gk_up, gla_gk_b, gla_norm_w, w_branch, w_out, post_g, post_b):
    bsz, seq, d = x.shape
    depth = ada_w.shape[0]
    alpha = (2.0 * depth) ** 0.25
    mod_all = _adaln(c, ada_w, ada_b).reshape(depth, bsz, 3, d)
    x2 = x.reshape(bsz * seq, d)
    for l in range(depth):
        p = _layer_params(l, w_in, rw_mu, rw_w0, rw_w_up, rw_a0, rw_a_up, rw_k_k, rw_k_a,
                          rw_r_k, rw_ln_g, rw_ln_b, m2_conv_w, m2_conv_b, m2_dt_bias, m2_a_log,
                          m2_d_skip, m2_norm_w, gla_gk_up, gla_gk_b, gla_norm_w, w_branch, w_out,
                          post_g, post_b)
        mod = mod_all[l]
        proj = _inproj(x2, mod, p["w_in"], seq)
        y_rw, y_m2, y_gla = _mixers(proj, p, bsz, seq)
        x2 = _merge(y_rw, y_m2, y_gla, proj, x2, mod, p, seq, alpha)
    return x2.reshape(bsz, seq, d)
```

```python
import functools

import jax
import jax.numpy as jnp
from jax import lax
from jax.experimental import pallas as pl
from jax.experimental.pallas import tpu as pltpu

F32 = jnp.float32
BF16 = jnp.bfloat16
HIGHEST = lax.Precision.HIGHEST

D_MODEL = 1024
WIDTH = 512
LN_EPS = 1e-5
RW_HEADS, RW_DIM, RW_RANK = 8, 64, 64
RW_GN_EPS = 64e-5
RW_CHUNK = 64
RW_SUBCHUNKS = 4
RW_BLOCK = 2 * RW_CHUNK
M2_HEADS, M2_DIM, M2_GROUPS, M2_STATE, M2_CONV = 8, 64, 2, 128, 4
M2_CHUNK = 128
M2_EPS = 1e-5
GLA_HEADS, GLA_DK, GLA_DV, GLA_RANK = 4, 64, 128, 16
GLA_TAU = 16.0
GLA_CHUNK = 64
GLA_SUBCHUNKS = 2
GLA_EPS = 1e-5
N_BRANCH = 3
MIX_TILE = 512

LANE = 128
SUBLANE = 8
MXU_DIM = 256

SEG512 = ("rw_r", "rw_k", "rw_v", "rw_gate", "m2_x", "m2_bc", "m2_z", "gla_qk", "gla_v", "gla_gate")
SEG128 = ("rw_wa", "m2_dt", "gla_gk")
COL512_BASE = N_BRANCH * D_MODEL
COL128_BASE = COL512_BASE + WIDTH * len(SEG512)
PROJ_WIDTH = -(-(COL128_BASE + LANE * len(SEG128)) // WIDTH) * WIDTH
PROJ_COL_TILES = 4
PROJ_TILE_N = PROJ_WIDTH // PROJ_COL_TILES
assert PROJ_TILE_N * PROJ_COL_TILES == PROJ_WIDTH and PROJ_TILE_N % LANE == 0

P_M2 = 1
P_GLA = 1
P_LOWRANK = 3
HEAD_SUM_PIECES = 1

NN = ((1,), (0,))
NT = ((1,), (1,))
TN = ((0,), (0,))


def _blk512(name):
    return (COL512_BASE + WIDTH * SEG512.index(name)) // WIDTH


def _blk128(name):
    return (COL128_BASE + LANE * SEG128.index(name)) // LANE


def _bdot(a, b):
    return jnp.dot(a.astype(BF16), b.astype(BF16), preferred_element_type=F32)


def _dg(a, b, dims):
    return lax.dot_general(a, b, (dims, ((), ())), preferred_element_type=F32)


def _split2(x):
    hi = x.astype(BF16)
    return hi, (x - hi.astype(F32)).astype(BF16)


def _dot(a, b, dims=NN, passes=1):
    if passes == 1:
        return _dg(a.astype(BF16), b.astype(BF16), dims)
    a_hi, a_lo = _split2(a)
    b_hi, b_lo = _split2(b)
    return _dg(a_hi, b_hi, dims) + _dg(a_lo, b_hi, dims) + _dg(a_hi, b_lo, dims)


def _dot_exact_rhs(a, b_bf16, pieces=2):
    out = None
    for _ in range(pieces):
        piece = a.astype(BF16)
        a = a - piece.astype(F32)
        term = _dg(piece, b_bf16, NN)
        out = term if out is None else out + term
    return out


def _chunk_cumsum(x, chunk):
    pos = lax.broadcasted_iota(jnp.int32, x.shape, 0) & (chunk - 1)
    step = 1
    while step < chunk:
        x = x + jnp.where(pos >= step, pltpu.roll(x, step, 0), 0.0)
        step *= 2
    return x


def _sigmoid(x):
    return jax.nn.sigmoid(x)


def _silu(x):
    return x * jax.nn.sigmoid(x)


def _softplus(x):
    return jnp.maximum(x, 0.0) + jnp.log1p(jnp.exp(-jnp.abs(x)))


def _tri(n):
    return (lax.broadcasted_iota(jnp.int32, (n, n), 0)
            >= lax.broadcasted_iota(jnp.int32, (n, n), 1))


def _shift_rows(x, prev, j):
    ext = jnp.concatenate([prev, x], axis=0)
    return pltpu.roll(ext, j, 0)[SUBLANE:, :]


def _emit_interleaved(stages, fillers):
    n_s, n_f = len(stages), len(fillers)
    after = [((k + 1) * n_s) // (n_f + 1) for k in range(n_f)]
    k = 0
    for i, stage in enumerate(stages):
        stage()
        while k < n_f and after[k] <= i + 1:
            fillers[k]()
            k += 1
    for filler in fillers[k:]:
        filler()


def _adaln_kernel(c_ref, w_ref, b_ref, o_ref):
    o_ref[...] = lax.dot_general(_silu(c_ref[...]), w_ref[...], (NN, ((), ())), precision=HIGHEST,
                                 preferred_element_type=F32) + b_ref[...]


def _adaln(c, ada_w, ada_b):
    depth, d, n = ada_w.shape
    bsz = c.shape[0]
    tn = 768
    return pl.pallas_call(
        _adaln_kernel,
        grid=(depth, n // tn),
        in_specs=[pl.BlockSpec((bsz, d), lambda l, j: (0, 0)),
                  pl.BlockSpec((None, d, tn), lambda l, j: (l, 0, j)),
                  pl.BlockSpec((None, 1, tn), lambda l, j: (l, 0, j))],
        out_specs=pl.BlockSpec((None, bsz, tn), lambda l, j: (l, 0, j)),
        out_shape=jax.ShapeDtypeStruct((depth, bsz, n), F32),
        name="adaln_mod",
    )(c, ada_w, ada_b.reshape(depth, 1, n))


def _inproj_kernel(x_ref, mod_ref, w_ref, o_ref, h_scr):
    @pl.when(pl.program_id(1) == 0)
    def _():
        x = x_ref[...]
        mu = jnp.mean(x, axis=-1, keepdims=True)
        xc = x - mu
        var = jnp.mean(xc * xc, axis=-1, keepdims=True)
        h = xc * lax.rsqrt(var + LN_EPS) * (1.0 + mod_ref[1:2, :]) + mod_ref[0:1, :]
        h_scr[...] = h.astype(BF16)

    o_ref[...] = jnp.dot(h_scr[...], w_ref[...], preferred_element_type=F32)


def _inproj(x2, mod, w_bf16, seq):
    t, d = x2.shape
    n = w_bf16.shape[1]
    tm = min(1024, seq)
    tiles_per_seq = seq // tm
    return pl.pallas_call(
        _inproj_kernel,
        grid=(t // tm, n // PROJ_TILE_N),
        in_specs=[pl.BlockSpec((tm, d), lambda i, j: (i, 0)),
                  pl.BlockSpec((None, 3, d), lambda i, j: (i // tiles_per_seq, 0, 0)),
                  pl.BlockSpec((d, PROJ_TILE_N), lambda i, j: (0, j))],
        out_specs=pl.BlockSpec((tm, PROJ_TILE_N), lambda i, j: (i, j)),
        out_shape=jax.ShapeDtypeStruct((t, n), F32),
        scratch_shapes=[pltpu.VMEM((tm, d), BF16)],
        compiler_params=pltpu.CompilerParams(dimension_semantics=("parallel", "arbitrary")),
        name="inproj",
    )(x2, mod, w_bf16)


def _rwkv_plan(tb, r_ref, k_ref, v_ref, g_ref, wa_ref, mu_r, mu_k, mu_v, mu_wa, w0, w_up, a0, a_up,
               k_k, k_a, r_k, ln_g, ln_b, ones_bd, o_ref, car_r, car_k, car_v, car_wa, st):
    L, N, RB = RW_CHUNK, RW_DIM, RW_BLOCK
    nchunk, nblock = tb // L, tb // RB
    blocks = [dict() for _ in range(nblock)]
    ys = [[None] * (RW_HEADS // 2) for _ in range(nchunk)]

    def head_sum(t):
        parts = [_dot_exact_rhs(t[:, c:c + MXU_DIM], ones_bd[...], HEAD_SUM_PIECES)
                 for c in range(0, WIDTH, MXU_DIM)]
        return jnp.concatenate(parts, axis=1)

    def prepare(bi):
        d = blocks[bi]
        rows = slice(bi * RB, (bi + 1) * RB)

        def lerp(x_ref, car, mu):
            x = x_ref[rows, :]
            prev = car[...] if bi == 0 else x_ref[bi * RB - SUBLANE:bi * RB, :]
            if bi == nblock - 1:
                car[...] = x[RB - SUBLANE:, :]
            return x + mu[...] * (_shift_rows(x, prev, 1) - x)

        def p_rk():
            d["r"] = lerp(r_ref, car_r, mu_r)
            d["k"] = lerp(k_ref, car_k, mu_k)

        def p_v():
            d["v"] = lerp(v_ref, car_v, mu_v)
            d["wa"] = lerp(wa_ref, car_wa, mu_wa)

        def p_decay():
            w = w0[...] + _dot(jnp.tanh(d["wa"]), w_up[...], NN, P_LOWRANK)
            d["lw"] = -0.6065306597126334 * _sigmoid(w)
            d["a"] = _sigmoid(a0[...] + _dot(d["wa"], a_up[...], NN, P_LOWRANK))

        def p_kk():
            kk = d["k"] * k_k[...]
            d["kk"] = kk * lax.rsqrt(jnp.maximum(head_sum(kk * kk), 1e-24))

        def p_k2():
            d["k2"] = d["k"] * (1.0 + (d["a"] - 1.0) * k_a[...])
            d["bonus"] = head_sum(d["r"] * d["k2"] * r_k[...]) * d["v"]
            d["b"] = d["kk"] * d["a"]

        def p_cumsum():
            d["g"] = _chunk_cumsum(d["lw"], L)

        def p_scale():
            g = d["g"]
            eng = jnp.exp(-g)
            d["at"] = -d["kk"] * jnp.exp(g - d["lw"])
            d["bt"] = d["b"] * eng
            d["kt"] = d["k2"] * eng
            d["rt"] = d["r"] * jnp.exp(g)

        return [p_rk, p_v, p_decay, p_kk, p_k2, p_cumsum, p_scale]

    P2 = 2 * L
    npair = RW_HEADS // 2
    lane_lo = lax.broadcasted_iota(jnp.int32, (L, P2), 1) < N
    zero_b = jnp.zeros((L, P2), BF16)

    def hat(xb):
        return jnp.concatenate([jnp.where(lane_lo, xb, zero_b), jnp.where(lane_lo, zero_b, xb)],
                               axis=0)

    bf = lambda x: x.astype(BF16)
    ri = lax.broadcasted_iota(jnp.int32, (P2, 2 * P2), 0)
    ci = lax.broadcasted_iota(jnp.int32, (P2, 2 * P2), 1)
    tpos = ri & (L - 1)
    spos = ci & (L - 1)
    mask_ar = jnp.where(ri < L, tpos, tpos + 1) > spos
    eye_side = (lax.broadcasted_iota(jnp.int32, (L, P2), 0)
                == (lax.broadcasted_iota(jnp.int32, (L, P2), 1) & (L - 1))).astype(F32)
    eye = (lax.broadcasted_iota(jnp.int32, (P2, P2), 0)
           == lax.broadcasted_iota(jnp.int32, (P2, P2), 1)).astype(F32)

    def group(chunks):
        probs = [dict(i=i, j=j) for i in chunks for j in range(npair)]

        def load(i):
            d = blocks[(i * L) // RB]
            rows = slice((i * L) % RB, (i * L) % RB + L)
            gc = d["g"][rows]
            gl = gc[L - 1:L, :]
            egl = jnp.exp(gl - gc)
            dec = jnp.exp(gl)
            bh = d["b"][rows] * egl
            kh = d["k2"][rows] * egl
            for q in probs:
                if q["i"] != i:
                    continue
                ls = slice(q["j"] * P2, (q["j"] + 1) * P2)
                q.update(dec=dec[:, ls], r=d["rt"][rows, ls], rb=bf(d["rt"][rows, ls]),
                         ab=bf(d["at"][rows, ls]), ha=hat(bf(d["at"][rows, ls])),
                         hb=hat(bf(d["bt"][rows, ls])), hk=hat(bf(d["kt"][rows, ls])),
                         hv=hat(bf(d["v"][rows, ls])), hbh=hat(bf(bh[:, ls])),
                         hkh=hat(bf(kh[:, ls])))

        def interactions(q):
            m = _dg(jnp.concatenate([q["ab"], q["rb"]], axis=0),
                    jnp.concatenate([q["hb"], q["hk"]], axis=0), NT)
            m = jnp.where(mask_ar, m, 0.0)
            q["a_ak"] = bf(m[:L, P2:])
            q["a_r"] = bf(m[L:, :])
            q["p"] = m[:L, :P2]
            q["t"] = eye_side + q["p"]

        def square(q):
            pb = bf(q["p"])
            q["p"] = _dg(pb, hat(pb), NN)

        def double(q):
            pb = bf(q["p"])
            tp = _dg(pb, jnp.concatenate([hat(bf(q["t"])), hat(pb)], axis=1), NN)
            q["t"], q["p"] = q["t"] + tp[:, :P2], tp[:, P2:]

        def inverse_done(q):
            q["t"] = q["t"] + _dg(bf(q["p"]), hat(bf(q["t"])), NN)

        def av(q):
            q["av"] = _dg(q["a_ak"], q["hv"], NN)

        def tu(q):
            out = _dg(bf(q["t"]), jnp.concatenate([q["ha"], hat(bf(q["av"]))], axis=1), NN)
            q["hwm"], q["hu0"] = hat(bf(out[:, :P2])), hat(bf(out[:, P2:]))

        def ru(q):
            rhs = jnp.concatenate(
                [jnp.concatenate([q["hwm"], q["hu0"]], axis=1),
                 jnp.concatenate([jnp.zeros((P2, P2), BF16), q["hv"]], axis=1)], axis=0)
            out = _dg(q["a_r"], rhs, NN)
            q["rm"] = bf(q["r"] + out[:, :P2])
            q["y0"] = out[:, P2:]

        def maps(q):
            q["mt"] = bf(eye * q["dec"] + _dg(q["hwm"], q["hbh"], TN))
            q["cc"] = _dg(jnp.concatenate([q["hu0"], q["hv"]], axis=0),
                          jnp.concatenate([q["hbh"], q["hkh"]], axis=0), TN)

        def recur(q):
            s0 = bf(st[q["j"]])
            ys[q["i"]][q["j"]] = _dg(q["rm"], s0, NT) + q["y0"]
            st[q["j"]] = _dg(s0, q["mt"], NN) + q["cc"]

        pieces = [functools.partial(load, i) for i in chunks]
        for stage in (interactions, square, double, double, double, double, inverse_done,
                      av, tu, ru, maps, recur):
            pieces += [functools.partial(stage, q) for q in probs]
        return pieces

    def close(bi):
        d = blocks[bi]
        rows = slice(bi * RB, (bi + 1) * RB)

        def c_mean():
            y = jnp.concatenate([jnp.concatenate(ys[i], axis=1)
                                 for i in range(bi * RB // L, (bi + 1) * RB // L)], axis=0)
            d["yc"] = y - head_sum(y) * (1.0 / N)

        def c_var():
            d["var"] = head_sum(d["yc"] * d["yc"]) * (1.0 / N)

        def c_out():
            yn = d["yc"] * lax.rsqrt(d["var"] + RW_GN_EPS) * ln_g[...] + ln_b[...]
            o_ref[rows, :] = ((yn + d["bonus"]) * _silu(g_ref[rows, :])).astype(o_ref.dtype)

        return [c_mean, c_var, c_out]

    groups = [group(range(c0, c0 + RW_SUBCHUNKS)) for c0 in range(0, nchunk, RW_SUBCHUNKS)]
    return ([prepare(bi) for bi in range(nblock)], groups, [close(bi) for bi in range(nblock)])


def _mamba_pieces(tb, x_ref, bc_ref, z_ref, dt_ref, cw_x, cw_bc, cb_x, cb_bc, dt_bias, a_log,
                  d_skip, norm_w, expand, o_ref, car_x, car_bc, st, y_scr):
    L = M2_CHUNK
    nchunk = tb // L
    gw = WIDTH // M2_GROUPS
    hpg = M2_HEADS // M2_GROUPS
    tri_incl = _tri(L)

    def chunk(c):
        d = {}
        rows = slice(c * L, (c + 1) * L)

        def conv(ref, car, w, bias):
            x = ref[rows, :]
            prev = car[...] if c == 0 else ref[c * L - SUBLANE:c * L, :]
            y = x * w[M2_CONV - 1:M2_CONV, :] + bias[...]
            for j in range(1, M2_CONV):
                y = y + _shift_rows(x, prev, j) * w[M2_CONV - 1 - j:M2_CONV - j, :]
            if c == nchunk - 1:
                car[...] = x[L - SUBLANE:, :]
            return _silu(y)

        def m_conv_x():
            d["xs"] = conv(x_ref, car_x, cw_x, cb_x)

        def m_conv_bc():
            bcc = conv(bc_ref, car_bc, cw_bc, cb_bc)
            d["bm"] = [bcc[:, g * M2_STATE:(g + 1) * M2_STATE] for g in range(M2_GROUPS)]
            d["cm"] = [bcc[:, gw + g * M2_STATE:gw + (g + 1) * M2_STATE]
                       for g in range(M2_GROUPS)]

        def m_decay():
            ex = expand[...]
            dt = _softplus(dt_ref[rows, :] + dt_bias[...])
            a_c = _chunk_cumsum(dt * (-jnp.exp(a_log[...])), L)
            d["a_c"], d["a_t"] = a_c, a_c.T
            a_x = _dot_exact_rhs(a_c, ex, 3)
            d["xdt"] = d["xs"] * _dot_exact_rhs(dt, ex, 3)
            a_last = a_x[L - 1:L, :]
            d["xdt_end"] = d["xdt"] * jnp.exp(a_last - a_x)
            d["e_ax"] = jnp.exp(a_x)
            d["e_last"] = jnp.exp(a_last)

        def m_cb():
            d["cb"] = [_dot(d["cm"][g], d["bm"][g], NT, P_M2) for g in range(M2_GROUPS)]

        def m_head(h):
            diff = d["a_c"][:, h:h + 1] - d["a_t"][h:h + 1, :]
            dec = jnp.exp(jnp.where(tri_incl, diff, -jnp.inf))
            d["yd%d" % h] = _dot(d["cb"][h // hpg] * dec,
                                 d["xdt"][:, h * M2_DIM:(h + 1) * M2_DIM], NN, P_M2)

        def m_state(g):
            gs = slice(g * gw, (g + 1) * gw)
            kv = _dot(d["bm"][g], d["xdt_end"][:, gs], TN, P_M2)
            s0 = st[g]
            y_off = _dot(d["cm"][g], s0, NN, P_M2) * d["e_ax"][:, gs]
            st[g] = s0 * d["e_last"][:, gs] + kv
            for hh in range(hpg):
                h = g * hpg + hh
                y_scr[rows, h * M2_DIM:(h + 1) * M2_DIM] = (
                    d["yd%d" % h] + y_off[:, hh * M2_DIM:(hh + 1) * M2_DIM])

        def m_out():
            y = (y_scr[rows, :] + d["xs"] * d_skip[...]) * _silu(z_ref[rows, :])
            outs = []
            for g in range(M2_GROUPS):
                yg = y[:, g * gw:(g + 1) * gw]
                ms = jnp.mean(yg * yg, axis=-1, keepdims=True)
                outs.append(yg * lax.rsqrt(ms + M2_EPS))
            o_ref[rows, :] = (jnp.concatenate(outs, axis=-1) * norm_w[...]).astype(o_ref.dtype)

        return ([m_conv_x, m_conv_bc, m_decay, m_cb]
                + [functools.partial(m_head, h) for h in range(M2_HEADS)]
                + [functools.partial(m_state, g) for g in range(M2_GROUPS)] + [m_out])

    return [chunk(c) for c in range(nchunk)]


def _gla_pieces(tb, qk_ref, v_ref, g_ref, gk_ref, gk_up, gk_b, norm_w, o_ref, st):
    L = GLA_CHUNK
    span = L * GLA_SUBCHUNKS
    kd = GLA_HEADS * GLA_DK
    tri_incl = _tri(L)

    def block(i):
        d = {}
        rows = slice(i * span, (i + 1) * span)
        probs = [dict(sub=sub, h=h) for sub in range(GLA_SUBCHUNKS) for h in range(GLA_HEADS)]

        def g_gates():
            la = -_softplus(-(_dot(gk_ref[rows, :], gk_up[...], NN, P_LOWRANK) + gk_b[...]))
            g = _chunk_cumsum(la * (1.0 / GLA_TAU), L)
            qk = qk_ref[rows, :]
            d["g"], d["kraw"] = g, qk[:, kd:]
            d["qg"] = qk[:, :kd] * (GLA_DK ** -0.5) * jnp.exp(g)
            d["kg"] = d["kraw"] * jnp.exp(-g)

        def g_split():
            vv = v_ref[rows, :]
            for sub in range(GLA_SUBCHUNKS):
                rs = slice(sub * L, (sub + 1) * L)
                gc = d["g"][rs]
                gl = gc[L - 1:L, :]
                kh = d["kraw"][rs] * jnp.exp(gl - gc)
                dec = jnp.exp(gl)
                for q in probs:
                    if q["sub"] != sub:
                        continue
                    ks = slice(q["h"] * GLA_DK, (q["h"] + 1) * GLA_DK)
                    vs = slice(q["h"] * GLA_DV, (q["h"] + 1) * GLA_DV)
                    q.update(qg=d["qg"][rs, ks], kg=d["kg"][rs, ks], v=vv[rs, vs], kh=kh[:, ks],
                             dec=dec[:, ks])

        def g_att(sub):
            for q in probs:
                if q["sub"] == sub:
                    q["att"] = jnp.where(tri_incl, _dot(q["qg"], q["kg"], NT, P_GLA), 0.0)

        def g_av(sub):
            for q in probs:
                if q["sub"] == sub:
                    q["o"] = _dot(q["att"], q["v"], NN, P_GLA)
                    q["kv"] = _dot(q["v"], q["kh"], TN, P_GLA)

        def g_recur(sub):
            for q in probs:
                if q["sub"] == sub:
                    s0 = st[q["h"]]
                    oh = q["o"] + _dot(q["qg"], s0, NT, P_GLA)
                    st[q["h"]] = s0 * q["dec"] + q["kv"]
                    ms = jnp.mean(oh * oh, axis=-1, keepdims=True)
                    q["on"] = oh * lax.rsqrt(ms + GLA_EPS)

        def g_out():
            o = jnp.concatenate(
                [jnp.concatenate([q["on"] for q in probs if q["sub"] == sub], axis=1)
                 for sub in range(GLA_SUBCHUNKS)], axis=0)
            o_ref[rows, :] = (o * norm_w[...] * _silu(g_ref[rows, :])).astype(o_ref.dtype)

        subs = range(GLA_SUBCHUNKS)
        return ([g_gates, g_split] + [functools.partial(g_att, s) for s in subs]
                + [functools.partial(g_av, s) for s in subs]
                + [functools.partial(g_recur, s) for s in subs] + [g_out])

    return [block(i) for i in range(tb // span)]


def _mixers_kernel(r_ref, k_ref, v_ref, rg_ref, wa_ref, mx_ref, mbc_ref, mz_ref, mdt_ref,
                   gqk_ref, gv_ref, gg_ref, ggk_ref,
                   mu_r, mu_k, mu_v, mu_wa, w0, w_up, a0, a_up, k_k, k_a, r_k, ln_g, ln_b, ones_bd,
                   cw_x, cw_bc, cb_x, cb_bc, dt_bias, a_log, d_skip, m2_norm_w, expand,
                   gk_up, gk_b, gla_norm_w,
                   o_rw, o_m2, o_gla,
                   car_r, car_k, car_v, car_wa, st_rw, car_x, car_bc, st_m2, y_m2, st_gla):
    tb = r_ref.shape[0]

    @pl.when(pl.program_id(1) == 0)
    def _():
        for ref in (car_r, car_k, car_v, car_wa, st_rw, car_x, car_bc, st_m2, st_gla):
            ref[...] = jnp.zeros_like(ref)

    prepare, groups, close = _rwkv_plan(
        tb, r_ref, k_ref, v_ref, rg_ref, wa_ref, mu_r, mu_k, mu_v, mu_wa, w0, w_up, a0, a_up,
        k_k, k_a, r_k, ln_g, ln_b, ones_bd, o_rw, car_r, car_k, car_v, car_wa, st_rw)
    m2 = _mamba_pieces(tb, mx_ref, mbc_ref, mz_ref, mdt_ref, cw_x, cw_bc, cb_x, cb_bc, dt_bias,
                       a_log, d_skip, m2_norm_w, expand, o_m2, car_x, car_bc, st_m2, y_m2)
    gla = _gla_pieces(tb, gqk_ref, gv_ref, gg_ref, ggk_ref, gk_up, gk_b, gla_norm_w, o_gla,
                      st_gla)

    ngroup = len(groups)
    bpg = len(prepare) // ngroup
    share = lambda items, n: items[n * len(items) // ngroup:(n + 1) * len(items) // ngroup]
    flat = lambda lists: [piece for pieces in lists for piece in pieces]
    for piece in flat(prepare[:bpg]):
        piece()
    for n in range(ngroup):
        fillers = flat(prepare[(n + 1) * bpg:(n + 2) * bpg])
        if n > 0:
            fillers += flat(close[(n - 1) * bpg:n * bpg])
        side = [flat(share(m2, n)), flat(share(gla, n))]
        fillers += [piece for k in range(max(map(len, side))) for s in side for piece in s[k:k + 1]]
        _emit_interleaved(groups[n], fillers)
    for piece in flat(close[(ngroup - 1) * bpg:]):
        piece()


def _mixers(proj, p, bsz, seq):
    tb = min(MIX_TILE, seq)
    assert tb % (RW_SUBCHUNKS * RW_CHUNK) == 0 and tb % M2_CHUNK == 0
    spt = seq // tb
    row = lambda b, s: b * spt + s
    seg = lambda name: pl.BlockSpec((tb, WIDTH), lambda b, s, j=_blk512(name): (row(b, s), j))
    low = lambda name: pl.BlockSpec((tb, LANE), lambda b, s, j=_blk128(name): (row(b, s), j))
    vec = lambda n: pl.BlockSpec((1, n), lambda b, s: (0, 0))
    mat = lambda m, n: pl.BlockSpec((m, n), lambda b, s: (0, 0))
    out = pl.BlockSpec((tb, WIDTH), lambda b, s: (row(b, s), 0))
    kd = GLA_HEADS * GLA_DK
    carry = lambda n: pltpu.VMEM((SUBLANE, n), F32)
    y_shape = jax.ShapeDtypeStruct((bsz * seq, WIDTH), BF16)
    return pl.pallas_call(
        _mixers_kernel,
        grid=(bsz, spt),
        in_specs=[seg("rw_r"), seg("rw_k"), seg("rw_v"), seg("rw_gate"), low("rw_wa"),
                  seg("m2_x"), seg("m2_bc"), seg("m2_z"), low("m2_dt"),
                  seg("gla_qk"), seg("gla_v"), seg("gla_gate"), low("gla_gk"),
                  vec(WIDTH), vec(WIDTH), vec(WIDTH), vec(LANE), vec(WIDTH), mat(LANE, WIDTH),
                  vec(WIDTH), mat(LANE, WIDTH), vec(WIDTH), vec(WIDTH), vec(WIDTH), vec(WIDTH),
                  vec(WIDTH), mat(MXU_DIM, MXU_DIM),
                  mat(M2_CONV, WIDTH), mat(M2_CONV, WIDTH), vec(WIDTH), vec(WIDTH),
                  vec(LANE), vec(LANE), vec(WIDTH), vec(WIDTH), mat(LANE, WIDTH),
                  mat(LANE, kd), vec(kd), vec(WIDTH)],
        out_specs=[out, out, out],
        out_shape=[y_shape, y_shape, y_shape],
        scratch_shapes=[carry(WIDTH), carry(WIDTH), carry(WIDTH), carry(LANE),
                        pltpu.VMEM((RW_HEADS // 2, 2 * RW_DIM, 2 * RW_DIM), F32),
                        carry(WIDTH), carry(WIDTH),
                        pltpu.VMEM((M2_GROUPS, M2_STATE, WIDTH // M2_GROUPS), F32),
                        pltpu.VMEM((tb, WIDTH), F32),
                        pltpu.VMEM((GLA_HEADS, GLA_DV, GLA_DK), F32)],
        compiler_params=pltpu.CompilerParams(dimension_semantics=("arbitrary", "arbitrary")),
        name="mixers",
    )(*([proj] * 13),
      p["mu_r"], p["mu_k"], p["mu_v"], p["mu_wa"], p["w0"], p["w_up"], p["a0"], p["a_up"],
      p["k_k"], p["k_a"], p["r_k"], p["ln_g"], p["ln_b"], p["ones_bd"],
      p["cw_x"], p["cw_bc"], p["cb_x"], p["cb_bc"], p["dt_bias"], p["a_log"], p["d_skip"],
      p["m2_norm_w"], p["expand"],
      p["gk_up"], p["gk_b"], p["gla_norm_w"])


def _merge_kernel(alpha, y_rw, y_m2, y_gla, lg0, lg1, lg2, x_ref, mod_ref, wb, w_out, pg, pb,
                  o_ref):
    merged = (_sigmoid(lg0[...]) * _bdot(y_rw[...], wb[0])
              + _sigmoid(lg1[...]) * _bdot(y_m2[...], wb[1])
              + _sigmoid(lg2[...]) * _bdot(y_gla[...], wb[2]))
    y = _bdot(merged, w_out[...])
    res = alpha * x_ref[...] + (1.0 + mod_ref[2:3, :]) * y
    mu = jnp.mean(res, axis=-1, keepdims=True)
    rc = res - mu
    var = jnp.mean(rc * rc, axis=-1, keepdims=True)
    o_ref[...] = rc * lax.rsqrt(var + LN_EPS) * pg[...] + pb[...]


def _merge(y_rw, y_m2, y_gla, proj, x2, mod, p, seq, alpha):
    t, d = x2.shape
    tm = min(512, seq)
    tiles_per_seq = seq // tm
    ytile = lambda: pl.BlockSpec((tm, WIDTH), lambda i: (i, 0))
    logit = lambda j: pl.BlockSpec((tm, d), lambda i, j=j: (i, j))
    return pl.pallas_call(
        functools.partial(_merge_kernel, alpha),
        grid=(t // tm,),
        in_specs=[ytile(), ytile(), ytile(), logit(0), logit(1), logit(2),
                  pl.BlockSpec((tm, d), lambda i: (i, 0)),
                  pl.BlockSpec((None, 3, d), lambda i: (i // tiles_per_seq, 0, 0)),
                  pl.BlockSpec((N_BRANCH, WIDTH, d), lambda i: (0, 0, 0)),
                  pl.BlockSpec((d, d), lambda i: (0, 0)),
                  pl.BlockSpec((1, d), lambda i: (0, 0)),
                  pl.BlockSpec((1, d), lambda i: (0, 0))],
        out_specs=pl.BlockSpec((tm, d), lambda i: (i, 0)),
        out_shape=jax.ShapeDtypeStruct((t, d), F32),
        compiler_params=pltpu.CompilerParams(dimension_semantics=("parallel",)),
        name="merge_out",
    )(y_rw, y_m2, y_gla, proj, proj, proj, x2, mod, p["w_branch"], p["w_out"], p["post_g"],
      p["post_b"])


def _pad_cols(a, n):
    return jnp.pad(a, ((0, 0), (0, n - a.shape[1])))


def _pad_rows_at(a, start, n):
    return jnp.pad(a, ((start, n - start - a.shape[0]), (0, 0)))


def _layer_params(l, w_in, rw_mu, rw_w0, rw_w_up, rw_a0, rw_a_up, rw_k_k, rw_k_a, rw_r_k,
                  rw_ln_g, rw_ln_b, m2_conv_w, m2_conv_b, m2_dt_bias, m2_a_log, m2_d_skip,
                  m2_norm_w, gla_gk_up, gla_gk_b, gla_norm_w, w_branch, w_out, post_g, post_b):
    W = WIDTH
    rank2 = 2 * RW_RANK
    sizes = (3 * W + rank2, W, W + 2 * M2_GROUPS * M2_STATE, M2_HEADS, W,
             GLA_HEADS * GLA_DK, GLA_HEADS * GLA_DK, GLA_HEADS * GLA_DV, GLA_RANK, W,
             N_BRANCH * D_MODEL)
    offs = [0]
    for s in sizes:
        offs.append(offs[-1] + s)
    (o_rw, o_rwg, o_xbc, o_dt, o_z, o_q, o_k, o_v, o_gk, o_gg, o_merge, _) = offs
    wl = w_in[l]
    cols = lambda a, n: wl[:, a:a + n]
    segs = {
        "rw_r": cols(o_rw, W), "rw_k": cols(o_rw + W, W), "rw_v": cols(o_rw + 2 * W, W),
        "rw_gate": cols(o_rwg, W), "m2_x": cols(o_xbc, W), "m2_bc": cols(o_xbc + W, W),
        "m2_z": cols(o_z, W), "gla_qk": cols(o_q, W), "gla_v": cols(o_v, W),
        "gla_gate": cols(o_gg, W),
        "rw_wa": cols(o_rw + 3 * W, rank2), "m2_dt": _pad_cols(cols(o_dt, M2_HEADS), LANE),
        "gla_gk": _pad_cols(cols(o_gk, GLA_RANK), LANE),
    }
    w_cat = jnp.concatenate([cols(o_merge, N_BRANCH * D_MODEL)] + [segs[n] for n in SEG512]
                            + [segs[n] for n in SEG128], axis=1)
    w_cat = _pad_cols(w_cat, PROJ_WIDTH).astype(BF16)

    row = lambda a: a.reshape(1, -1)
    mu = rw_mu[l]
    head_of_ch = jnp.arange(MXU_DIM) // RW_DIM
    ones_bd = (head_of_ch[:, None] == head_of_ch[None, :]).astype(BF16)
    expand = (jnp.arange(LANE)[:, None] == (jnp.arange(W) // M2_DIM)[None, :]).astype(BF16)
    cw = m2_conv_w[l]
    cb = m2_conv_b[l]
    return {
        "w_in": w_cat,
        "mu_r": row(mu[:W]), "mu_k": row(mu[W:2 * W]), "mu_v": row(mu[2 * W:3 * W]),
        "mu_wa": row(mu[3 * W:]),
        "w0": row(rw_w0[l]), "w_up": _pad_rows_at(rw_w_up[l], 0, LANE),
        "a0": row(rw_a0[l]), "a_up": _pad_rows_at(rw_a_up[l], RW_RANK, LANE),
        "k_k": row(rw_k_k[l]), "k_a": row(rw_k_a[l]), "r_k": row(rw_r_k[l]),
        "ln_g": row(rw_ln_g[l]), "ln_b": row(rw_ln_b[l]), "ones_bd": ones_bd,
        "cw_x": cw[:, :W], "cw_bc": cw[:, W:], "cb_x": row(cb[:W]), "cb_bc": row(cb[W:]),
        "dt_bias": _pad_cols(row(m2_dt_bias[l]), LANE), "a_log": _pad_cols(row(m2_a_log[l]), LANE),
        "d_skip": row(jnp.repeat(m2_d_skip[l], M2_DIM)), "m2_norm_w": row(m2_norm_w[l]),
        "expand": expand,
        "gk_up": _pad_rows_at(gla_gk_up[l], 0, LANE), "gk_b": row(gla_gk_b[l]),
        "gla_norm_w": row(jnp.tile(gla_norm_w[l], GLA_HEADS)),
        "w_branch": w_branch[l].astype(BF16), "w_out": w_out[l].astype(BF16),
        "post_g": row(post_g[l]), "post_b": row(post_b[l]),
    }


def kernel(x, c, ada_w, ada_b, w_in, rw_mu, rw_w0, rw_w_up, rw_a0, rw_a_up, rw_k_k, rw_k_a, rw_r_k, rw_ln_g, rw_ln_b, m2_conv_w, m2_conv_b, m2_dt_bias, m2_a_log, m2_d_skip, m2_norm_w, gla_gk_up, gla_gk_b, gla_norm_w, w_branch, w_out, post_g, post_b):
    bsz, seq, d = x.shape
    depth = ada_w.shape[0]
    alpha = (2.0 * depth) ** 0.25
    mod_all = _adaln(c, ada_w, ada_b).reshape(depth, bsz, 3, d)
    x2 = x.reshape(bsz * seq, d)
    for l in range(depth):
        p = _layer_params(l, w_in, rw_mu, rw_w0, rw_w_up, rw_a0, rw_a_up, rw_k_k, rw_k_a,
                          rw_r_k, rw_ln_g, rw_ln_b, m2_conv_w, m2_conv_b, m2_dt_bias, m2_a_log,
                          m2_d_skip, m2_norm_w, gla_gk_up, gla_gk_b, gla_norm_w, w_branch, w_out,
                          post_g, post_b)
        mod = mod_all[l]
        proj = _inproj(x2, mod, p["w_in"], seq)
        y_rw, y_m2, y_gla = _mixers(proj, p, bsz, seq)
        x2 = _merge(y_rw, y_m2, y_gla, proj, x2, mod, p, seq, alpha)
    return x2.reshape(bsz, seq, d)
```

```python
import functools

import jax
import jax.numpy as jnp
from jax import lax
from jax.experimental import pallas as pl
from jax.experimental.pallas import tpu as pltpu

F32 = jnp.float32
BF16 = jnp.bfloat16
HIGHEST = lax.Precision.HIGHEST

D_MODEL = 1024
WIDTH = 512
LN_EPS = 1e-5
RW_HEADS, RW_DIM, RW_RANK = 8, 64, 64
RW_GN_EPS = 64e-5
RW_CHUNK = 64
RW_SUBCHUNKS = 4
RW_BLOCK = 2 * RW_CHUNK
M2_HEADS, M2_DIM, M2_GROUPS, M2_STATE, M2_CONV = 8, 64, 2, 128, 4
M2_CHUNK = 128
M2_EPS = 1e-5
GLA_HEADS, GLA_DK, GLA_DV, GLA_RANK = 4, 64, 128, 16
GLA_TAU = 16.0
GLA_CHUNK = 64
GLA_SUBCHUNKS = 2
GLA_EPS = 1e-5
N_BRANCH = 3
MIX_TILE = 512

LANE = 128
SUBLANE = 8
MXU_DIM = 256

SEG512 = ("rw_r", "rw_k", "rw_v", "rw_gate", "m2_x", "m2_bc", "m2_z", "gla_qk", "gla_v", "gla_gate")
SEG128 = ("rw_wa", "m2_dt", "gla_gk")
COL128_BASE = WIDTH * len(SEG512)
LOGIT_WIDTH = N_BRANCH * D_MODEL
LOGIT_COL_TILES = 2
LOGIT_TILE_N = LOGIT_WIDTH // LOGIT_COL_TILES
MIXIN_COL_TILES = 4
MIXIN_WIDTH = -(-(COL128_BASE + LANE * len(SEG128)) // (LANE * MIXIN_COL_TILES)) * LANE * MIXIN_COL_TILES
MIXIN_TILE_N = MIXIN_WIDTH // MIXIN_COL_TILES
assert LOGIT_TILE_N * LOGIT_COL_TILES == LOGIT_WIDTH and LOGIT_TILE_N % LANE == 0

P_M2 = 1
P_GLA = 1
P_LOWRANK = 3
HEAD_SUM_PIECES = 1

NN = ((1,), (0,))
NT = ((1,), (1,))
TN = ((0,), (0,))


def _blk512(name):
    return SEG512.index(name)


def _blk128(name):
    return (COL128_BASE + LANE * SEG128.index(name)) // LANE


def _bdot(a, b):
    return jnp.dot(a.astype(BF16), b.astype(BF16), preferred_element_type=F32)


def _dg(a, b, dims):
    return lax.dot_general(a, b, (dims, ((), ())), preferred_element_type=F32)


def _split2(x):
    hi = x.astype(BF16)
    return hi, (x - hi.astype(F32)).astype(BF16)


def _dot(a, b, dims=NN, passes=1):
    if passes == 1:
        return _dg(a.astype(BF16), b.astype(BF16), dims)
    a_hi, a_lo = _split2(a)
    b_hi, b_lo = _split2(b)
    return _dg(a_hi, b_hi, dims) + _dg(a_lo, b_hi, dims) + _dg(a_hi, b_lo, dims)


def _dot_exact_rhs(a, b_bf16, pieces=2):
    out = None
    for _ in range(pieces):
        piece = a.astype(BF16)
        a = a - piece.astype(F32)
        term = _dg(piece, b_bf16, NN)
        out = term if out is None else out + term
    return out


def _chunk_cumsum(x, chunk):
    pos = lax.broadcasted_iota(jnp.int32, x.shape, 0) & (chunk - 1)
    step = 1
    while step < chunk:
        x = x + jnp.where(pos >= step, pltpu.roll(x, step, 0), 0.0)
        step *= 2
    return x


def _sigmoid(x):
    return jax.nn.sigmoid(x)


def _silu(x):
    return x * jax.nn.sigmoid(x)


def _softplus(x):
    return jnp.maximum(x, 0.0) + jnp.log1p(jnp.exp(-jnp.abs(x)))


def _tri(n):
    return (lax.broadcasted_iota(jnp.int32, (n, n), 0)
            >= lax.broadcasted_iota(jnp.int32, (n, n), 1))


def _shift_rows(x, prev, j):
    ext = jnp.concatenate([prev, x], axis=0)
    return pltpu.roll(ext, j, 0)[SUBLANE:, :]


def _emit_interleaved(stages, fillers):
    n_s, n_f = len(stages), len(fillers)
    after = [((k + 1) * n_s) // (n_f + 1) for k in range(n_f)]
    k = 0
    for i, stage in enumerate(stages):
        stage()
        while k < n_f and after[k] <= i + 1:
            fillers[k]()
            k += 1
    for filler in fillers[k:]:
        filler()


def _adaln_kernel(c_ref, w_ref, b_ref, o_ref):
    o_ref[...] = lax.dot_general(_silu(c_ref[...]), w_ref[...], (NN, ((), ())), precision=HIGHEST,
                                 preferred_element_type=F32) + b_ref[...]


def _adaln(c, ada_w, ada_b):
    depth, d, n = ada_w.shape
    bsz = c.shape[0]
    tn = 768
    return pl.pallas_call(
        _adaln_kernel,
        grid=(depth, n // tn),
        in_specs=[pl.BlockSpec((bsz, d), lambda l, j: (0, 0)),
                  pl.BlockSpec((None, d, tn), lambda l, j: (l, 0, j)),
                  pl.BlockSpec((None, 1, tn), lambda l, j: (l, 0, j))],
        out_specs=pl.BlockSpec((None, bsz, tn), lambda l, j: (l, 0, j)),
        out_shape=jax.ShapeDtypeStruct((depth, bsz, n), F32),
        name="adaln_mod",
    )(c, ada_w, ada_b.reshape(depth, 1, n))


def _inproj_kernel(x_ref, mod_ref, wl_ref, wm_ref, logit_ref, mixin_ref, h_scr):
    j = pl.program_id(1)

    @pl.when(j == 0)
    def _():
        x = x_ref[...]
        mu = jnp.mean(x, axis=-1, keepdims=True)
        xc = x - mu
        var = jnp.mean(xc * xc, axis=-1, keepdims=True)
        h = xc * lax.rsqrt(var + LN_EPS) * (1.0 + mod_ref[1:2, :]) + mod_ref[0:1, :]
        h_scr[...] = h.astype(BF16)

    @pl.when(j < LOGIT_COL_TILES)
    def _():
        logit_ref[...] = jnp.dot(h_scr[...], wl_ref[...],
                                 preferred_element_type=F32).astype(logit_ref.dtype)

    @pl.when(j >= LOGIT_COL_TILES)
    def _():
        mixin_ref[...] = jnp.dot(h_scr[...], wm_ref[...], preferred_element_type=F32)


def _inproj(x2, mod, w_logit, w_mixin, seq):
    t, d = x2.shape
    tm = min(1024, seq)
    tiles_per_seq = seq // tm
    lcol = lambda j: jnp.minimum(j, LOGIT_COL_TILES - 1)
    mcol = lambda j: jnp.maximum(j - LOGIT_COL_TILES, 0)
    return pl.pallas_call(
        _inproj_kernel,
        grid=(t // tm, LOGIT_COL_TILES + MIXIN_COL_TILES),
        in_specs=[pl.BlockSpec((tm, d), lambda i, j: (i, 0)),
                  pl.BlockSpec((None, 3, d), lambda i, j: (i // tiles_per_seq, 0, 0)),
                  pl.BlockSpec((d, LOGIT_TILE_N), lambda i, j: (0, lcol(j))),
                  pl.BlockSpec((d, MIXIN_TILE_N), lambda i, j: (0, mcol(j)))],
        out_specs=[pl.BlockSpec((tm, LOGIT_TILE_N), lambda i, j: (i, lcol(j))),
                   pl.BlockSpec((tm, MIXIN_TILE_N), lambda i, j: (i, mcol(j)))],
        out_shape=[jax.ShapeDtypeStruct((t, LOGIT_WIDTH), BF16),
                   jax.ShapeDtypeStruct((t, MIXIN_WIDTH), F32)],
        scratch_shapes=[pltpu.VMEM((tm, d), BF16)],
        compiler_params=pltpu.CompilerParams(dimension_semantics=("parallel", "arbitrary")),
        name="inproj",
    )(x2, mod, w_logit, w_mixin)


def _rwkv_plan(tb, r_ref, k_ref, v_ref, g_ref, wa_ref, mu_r, mu_k, mu_v, mu_wa, w0, w_up, a0, a_up,
               k_k, k_a, r_k, ln_g, ln_b, ones_bd, o_ref, car_r, car_k, car_v, car_wa, st):
    L, N, RB = RW_CHUNK, RW_DIM, RW_BLOCK
    nchunk, nblock = tb // L, tb // RB
    blocks = [dict() for _ in range(nblock)]
    ys = [[None] * (RW_HEADS // 2) for _ in range(nchunk)]

    def head_sum(t):
        parts = [_dot_exact_rhs(t[:, c:c + MXU_DIM], ones_bd[...], HEAD_SUM_PIECES)
                 for c in range(0, WIDTH, MXU_DIM)]
        return jnp.concatenate(parts, axis=1)

    def prepare(bi):
        d = blocks[bi]
        rows = slice(bi * RB, (bi + 1) * RB)

        def lerp(x_ref, car, mu):
            x = x_ref[rows, :]
            prev = car[...] if bi == 0 else x_ref[bi * RB - SUBLANE:bi * RB, :]
            if bi == nblock - 1:
                car[...] = x[RB - SUBLANE:, :]
            return x + mu[...] * (_shift_rows(x, prev, 1) - x)

        def p_rk():
            d["r"] = lerp(r_ref, car_r, mu_r)
            d["k"] = lerp(k_ref, car_k, mu_k)

        def p_v():
            d["v"] = lerp(v_ref, car_v, mu_v)
            d["wa"] = lerp(wa_ref, car_wa, mu_wa)

        def p_decay():
            w = w0[...] + _dot(jnp.tanh(d["wa"]), w_up[...], NN, P_LOWRANK)
            d["lw"] = -0.6065306597126334 * _sigmoid(w)
            d["a"] = _sigmoid(a0[...] + _dot(d["wa"], a_up[...], NN, P_LOWRANK))

        def p_kk():
            kk = d["k"] * k_k[...]
            d["kk"] = kk * lax.rsqrt(jnp.maximum(head_sum(kk * kk), 1e-24))

        def p_k2():
            d["k2"] = d["k"] * (1.0 + (d["a"] - 1.0) * k_a[...])
            d["bonus"] = head_sum(d["r"] * d["k2"] * r_k[...]) * d["v"]
            d["b"] = d["kk"] * d["a"]

        def p_cumsum():
            d["g"] = _chunk_cumsum(d["lw"], L)

        def p_scale():
            g = d["g"]
            eng = jnp.exp(-g)
            d["at"] = -d["kk"] * jnp.exp(g - d["lw"])
            d["bt"] = d["b"] * eng
            d["kt"] = d["k2"] * eng
            d["rt"] = d["r"] * jnp.exp(g)

        return [p_rk, p_v, p_decay, p_kk, p_k2, p_cumsum, p_scale]

    P2 = 2 * L
    npair = RW_HEADS // 2
    lane_lo = lax.broadcasted_iota(jnp.int32, (L, P2), 1) < N
    zero_b = jnp.zeros((L, P2), BF16)

    def hat(xb):
        return jnp.concatenate([jnp.where(lane_lo, xb, zero_b), jnp.where(lane_lo, zero_b, xb)],
                               axis=0)

    bf = lambda x: x.astype(BF16)
    ri = lax.broadcasted_iota(jnp.int32, (P2, 2 * P2), 0)
    ci = lax.broadcasted_iota(jnp.int32, (P2, 2 * P2), 1)
    tpos = ri & (L - 1)
    spos = ci & (L - 1)
    mask_ar = jnp.where(ri < L, tpos, tpos + 1) > spos
    eye_side = (lax.broadcasted_iota(jnp.int32, (L, P2), 0)
                == (lax.broadcasted_iota(jnp.int32, (L, P2), 1) & (L - 1))).astype(F32)
    eye = (lax.broadcasted_iota(jnp.int32, (P2, P2), 0)
           == lax.broadcasted_iota(jnp.int32, (P2, P2), 1)).astype(F32)

    def group(chunks):
        probs = [dict(i=i, j=j) for i in chunks for j in range(npair)]

        def load(i):
            d = blocks[(i * L) // RB]
            rows = slice((i * L) % RB, (i * L) % RB + L)
            gc = d["g"][rows]
            gl = gc[L - 1:L, :]
            egl = jnp.exp(gl - gc)
            dec = jnp.exp(gl)
            bh = d["b"][rows] * egl
            kh = d["k2"][rows] * egl
            for q in probs:
                if q["i"] != i:
                    continue
                ls = slice(q["j"] * P2, (q["j"] + 1) * P2)
                q.update(dec=dec[:, ls], r=d["rt"][rows, ls], rb=bf(d["rt"][rows, ls]),
                         ab=bf(d["at"][rows, ls]), ha=hat(bf(d["at"][rows, ls])),
                         hb=hat(bf(d["bt"][rows, ls])), hk=hat(bf(d["kt"][rows, ls])),
                         hv=hat(bf(d["v"][rows, ls])), hbh=hat(bf(bh[:, ls])),
                         hkh=hat(bf(kh[:, ls])))

        def interactions(q):
            m = _dg(jnp.concatenate([q["ab"], q["rb"]], axis=0),
                    jnp.concatenate([q["hb"], q["hk"]], axis=0), NT)
            m = jnp.where(mask_ar, m, 0.0)
            q["a_ak"] = bf(m[:L, P2:])
            q["a_r"] = bf(m[L:, :])
            q["p"] = m[:L, :P2]
            q["t"] = eye_side + q["p"]

        def square(q):
            pb = bf(q["p"])
            q["p"] = _dg(pb, hat(pb), NN)

        def double(q):
            pb = bf(q["p"])
            tp = _dg(pb, jnp.concatenate([hat(bf(q["t"])), hat(pb)], axis=1), NN)
            q["t"], q["p"] = q["t"] + tp[:, :P2], tp[:, P2:]

        def inverse_done(q):
            q["t"] = q["t"] + _dg(bf(q["p"]), hat(bf(q["t"])), NN)

        def av(q):
            q["av"] = _dg(q["a_ak"], q["hv"], NN)

        def tu(q):
            out = _dg(bf(q["t"]), jnp.concatenate([q["ha"], hat(bf(q["av"]))], axis=1), NN)
            q["hwm"], q["hu0"] = hat(bf(out[:, :P2])), hat(bf(out[:, P2:]))

        def ru(q):
            rhs = jnp.concatenate(
                [jnp.concatenate([q["hwm"], q["hu0"]], axis=1),
                 jnp.concatenate([jnp.zeros((P2, P2), BF16), q["hv"]], axis=1)], axis=0)
            out = _dg(q["a_r"], rhs, NN)
            q["rm"] = bf(q["r"] + out[:, :P2])
            q["y0"] = out[:, P2:]

        def maps(q):
            q["mt"] = bf(eye * q["dec"] + _dg(q["hwm"], q["hbh"], TN))
            q["cc"] = _dg(jnp.concatenate([q["hu0"], q["hv"]], axis=0),
                          jnp.concatenate([q["hbh"], q["hkh"]], axis=0), TN)

        def recur(q):
            s0 = bf(st[q["j"]])
            ys[q["i"]][q["j"]] = _dg(q["rm"], s0, NT) + q["y0"]
            st[q["j"]] = _dg(s0, q["mt"], NN) + q["cc"]

        pieces = [functools.partial(load, i) for i in chunks]
        for stage in (interactions, square, double, double, double, double, inverse_done,
                      av, tu, ru, maps, recur):
            pieces += [functools.partial(stage, q) for q in probs]
        return pieces

    def close(bi):
        d = blocks[bi]
        rows = slice(bi * RB, (bi + 1) * RB)

        def c_mean():
            y = jnp.concatenate([jnp.concatenate(ys[i], axis=1)
                                 for i in range(bi * RB // L, (bi + 1) * RB // L)], axis=0)
            d["yc"] = y - head_sum(y) * (1.0 / N)

        def c_var():
            d["var"] = head_sum(d["yc"] * d["yc"]) * (1.0 / N)

        def c_out():
            yn = d["yc"] * lax.rsqrt(d["var"] + RW_GN_EPS) * ln_g[...] + ln_b[...]
            o_ref[rows, :] = ((yn + d["bonus"]) * _silu(g_ref[rows, :])).astype(o_ref.dtype)

        return [c_mean, c_var, c_out]

    groups = [group(range(c0, c0 + RW_SUBCHUNKS)) for c0 in range(0, nchunk, RW_SUBCHUNKS)]
    return ([prepare(bi) for bi in range(nblock)], groups, [close(bi) for bi in range(nblock)])


def _mamba_pieces(tb, x_ref, bc_ref, z_ref, dt_ref, cw_x, cw_bc, cb_x, cb_bc, dt_bias, a_log,
                  d_skip, norm_w, expand, o_ref, car_x, car_bc, st, y_scr):
    L = M2_CHUNK
    nchunk = tb // L
    gw = WIDTH // M2_GROUPS
    hpg = M2_HEADS // M2_GROUPS
    tri_incl = _tri(L)

    def chunk(c):
        d = {}
        rows = slice(c * L, (c + 1) * L)

        def conv(ref, car, w, bias):
            x = ref[rows, :]
            prev = car[...] if c == 0 else ref[c * L - SUBLANE:c * L, :]
            y = x * w[M2_CONV - 1:M2_CONV, :] + bias[...]
            for j in range(1, M2_CONV):
                y = y + _shift_rows(x, prev, j) * w[M2_CONV - 1 - j:M2_CONV - j, :]
            if c == nchunk - 1:
                car[...] = x[L - SUBLANE:, :]
            return _silu(y)

        def m_conv_x():
            d["xs"] = conv(x_ref, car_x, cw_x, cb_x)

        def m_conv_bc():
            bcc = conv(bc_ref, car_bc, cw_bc, cb_bc)
            d["bm"] = [bcc[:, g * M2_STATE:(g + 1) * M2_STATE] for g in range(M2_GROUPS)]
            d["cm"] = [bcc[:, gw + g * M2_STATE:gw + (g + 1) * M2_STATE]
                       for g in range(M2_GROUPS)]

        def m_decay():
            ex = expand[...]
            dt = _softplus(dt_ref[rows, :] + dt_bias[...])
            a_c = _chunk_cumsum(dt * (-jnp.exp(a_log[...])), L)
            d["a_c"], d["a_t"] = a_c, a_c.T
            a_x = _dot_exact_rhs(a_c, ex, 3)
            d["xdt"] = d["xs"] * _dot_exact_rhs(dt, ex, 3)
            a_last = a_x[L - 1:L, :]
            d["xdt_end"] = d["xdt"] * jnp.exp(a_last - a_x)
            d["e_ax"] = jnp.exp(a_x)
            d["e_last"] = jnp.exp(a_last)

        def m_cb():
            d["cb"] = [_dot(d["cm"][g], d["bm"][g], NT, P_M2) for g in range(M2_GROUPS)]

        def m_head(h):
            diff = d["a_c"][:, h:h + 1] - d["a_t"][h:h + 1, :]
            dec = jnp.exp(jnp.where(tri_incl, diff, -jnp.inf))
            d["yd%d" % h] = _dot(d["cb"][h // hpg] * dec,
                                 d["xdt"][:, h * M2_DIM:(h + 1) * M2_DIM], NN, P_M2)

        def m_state(g):
            gs = slice(g * gw, (g + 1) * gw)
            kv = _dot(d["bm"][g], d["xdt_end"][:, gs], TN, P_M2)
            s0 = st[g]
            y_off = _dot(d["cm"][g], s0, NN, P_M2) * d["e_ax"][:, gs]
            st[g] = s0 * d["e_last"][:, gs] + kv
            for hh in range(hpg):
                h = g * hpg + hh
                y_scr[rows, h * M2_DIM:(h + 1) * M2_DIM] = (
                    d["yd%d" % h] + y_off[:, hh * M2_DIM:(hh + 1) * M2_DIM])

        def m_out():
            y = (y_scr[rows, :] + d["xs"] * d_skip[...]) * _silu(z_ref[rows, :])
            outs = []
            for g in range(M2_GROUPS):
                yg = y[:, g * gw:(g + 1) * gw]
                ms = jnp.mean(yg * yg, axis=-1, keepdims=True)
                outs.append(yg * lax.rsqrt(ms + M2_EPS))
            o_ref[rows, :] = (jnp.concatenate(outs, axis=-1) * norm_w[...]).astype(o_ref.dtype)

        return ([m_conv_x, m_conv_bc, m_decay, m_cb]
                + [functools.partial(m_head, h) for h in range(M2_HEADS)]
                + [functools.partial(m_state, g) for g in range(M2_GROUPS)] + [m_out])

    return [chunk(c) for c in range(nchunk)]


def _gla_pieces(tb, qk_ref, v_ref, g_ref, gk_ref, gk_up, gk_b, norm_w, o_ref, st):
    L = GLA_CHUNK
    span = L * GLA_SUBCHUNKS
    kd = GLA_HEADS * GLA_DK
    tri_incl = _tri(L)

    def block(i):
        d = {}
        rows = slice(i * span, (i + 1) * span)
        probs = [dict(sub=sub, h=h) for sub in range(GLA_SUBCHUNKS) for h in range(GLA_HEADS)]

        def g_gates():
            la = -_softplus(-(_dot(gk_ref[rows, :], gk_up[...], NN, P_LOWRANK) + gk_b[...]))
            g = _chunk_cumsum(la * (1.0 / GLA_TAU), L)
            qk = qk_ref[rows, :]
            d["g"], d["kraw"] = g, qk[:, kd:]
            d["qg"] = qk[:, :kd] * (GLA_DK ** -0.5) * jnp.exp(g)
            d["kg"] = d["kraw"] * jnp.exp(-g)

        def g_split():
            vv = v_ref[rows, :]
            for sub in range(GLA_SUBCHUNKS):
                rs = slice(sub * L, (sub + 1) * L)
                gc = d["g"][rs]
                gl = gc[L - 1:L, :]
                kh = d["kraw"][rs] * jnp.exp(gl - gc)
                dec = jnp.exp(gl)
                for q in probs:
                    if q["sub"] != sub:
                        continue
                    ks = slice(q["h"] * GLA_DK, (q["h"] + 1) * GLA_DK)
                    vs = slice(q["h"] * GLA_DV, (q["h"] + 1) * GLA_DV)
                    q.update(qg=d["qg"][rs, ks], kg=d["kg"][rs, ks], v=vv[rs, vs], kh=kh[:, ks],
                             dec=dec[:, ks])

        def g_att(sub):
            for q in probs:
                if q["sub"] == sub:
                    q["att"] = jnp.where(tri_incl, _dot(q["qg"], q["kg"], NT, P_GLA), 0.0)

        def g_av(sub):
            for q in probs:
                if q["sub"] == sub:
                    q["o"] = _dot(q["att"], q["v"], NN, P_GLA)
                    q["kv"] = _dot(q["v"], q["kh"], TN, P_GLA)

        def g_recur(sub):
            for q in probs:
                if q["sub"] == sub:
                    s0 = st[q["h"]]
                    oh = q["o"] + _dot(q["qg"], s0, NT, P_GLA)
                    st[q["h"]] = s0 * q["dec"] + q["kv"]
                    ms = jnp.mean(oh * oh, axis=-1, keepdims=True)
                    q["on"] = oh * lax.rsqrt(ms + GLA_EPS)

        def g_out():
            o = jnp.concatenate(
                [jnp.concatenate([q["on"] for q in probs if q["sub"] == sub], axis=1)
                 for sub in range(GLA_SUBCHUNKS)], axis=0)
            o_ref[rows, :] = (o * norm_w[...] * _silu(g_ref[rows, :])).astype(o_ref.dtype)

        subs = range(GLA_SUBCHUNKS)
        return ([g_gates, g_split] + [functools.partial(g_att, s) for s in subs]
                + [functools.partial(g_av, s) for s in subs]
                + [functools.partial(g_recur, s) for s in subs] + [g_out])

    return [block(i) for i in range(tb // span)]


def _mixers_kernel(r_ref, k_ref, v_ref, rg_ref, wa_ref, mx_ref, mbc_ref, mz_ref, mdt_ref,
                   gqk_ref, gv_ref, gg_ref, ggk_ref,
                   mu_r, mu_k, mu_v, mu_wa, w0, w_up, a0, a_up, k_k, k_a, r_k, ln_g, ln_b, ones_bd,
                   cw_x, cw_bc, cb_x, cb_bc, dt_bias, a_log, d_skip, m2_norm_w, expand,
                   gk_up, gk_b, gla_norm_w,
                   o_rw, o_m2, o_gla,
                   car_r, car_k, car_v, car_wa, st_rw, car_x, car_bc, st_m2, y_m2, st_gla):
    tb = r_ref.shape[0]

    @pl.when(pl.program_id(1) == 0)
    def _():
        for ref in (car_r, car_k, car_v, car_wa, st_rw, car_x, car_bc, st_m2, st_gla):
            ref[...] = jnp.zeros_like(ref)

    prepare, groups, close = _rwkv_plan(
        tb, r_ref, k_ref, v_ref, rg_ref, wa_ref, mu_r, mu_k, mu_v, mu_wa, w0, w_up, a0, a_up,
        k_k, k_a, r_k, ln_g, ln_b, ones_bd, o_rw, car_r, car_k, car_v, car_wa, st_rw)
    m2 = _mamba_pieces(tb, mx_ref, mbc_ref, mz_ref, mdt_ref, cw_x, cw_bc, cb_x, cb_bc, dt_bias,
                       a_log, d_skip, m2_norm_w, expand, o_m2, car_x, car_bc, st_m2, y_m2)
    gla = _gla_pieces(tb, gqk_ref, gv_ref, gg_ref, ggk_ref, gk_up, gk_b, gla_norm_w, o_gla,
                      st_gla)

    ngroup = len(groups)
    bpg = len(prepare) // ngroup
    share = lambda items, n: items[n * len(items) // ngroup:(n + 1) * len(items) // ngroup]
    flat = lambda lists: [piece for pieces in lists for piece in pieces]
    for piece in flat(prepare[:bpg]):
        piece()
    for n in range(ngroup):
        fillers = flat(prepare[(n + 1) * bpg:(n + 2) * bpg])
        if n > 0:
            fillers += flat(close[(n - 1) * bpg:n * bpg])
        side = [flat(share(m2, n)), flat(share(gla, n))]
        fillers += [piece for k in range(max(map(len, side))) for s in side for piece in s[k:k + 1]]
        _emit_interleaved(groups[n], fillers)
    for piece in flat(close[(ngroup - 1) * bpg:]):
        piece()


def _mixers(proj, p, bsz, seq):
    tb = min(MIX_TILE, seq)
    assert tb % (RW_SUBCHUNKS * RW_CHUNK) == 0 and tb % M2_CHUNK == 0
    spt = seq // tb
    row = lambda b, s: b * spt + s
    seg = lambda name: pl.BlockSpec((tb, WIDTH), lambda b, s, j=_blk512(name): (row(b, s), j))
    low = lambda name: pl.BlockSpec((tb, LANE), lambda b, s, j=_blk128(name): (row(b, s), j))
    vec = lambda n: pl.BlockSpec((1, n), lambda b, s: (0, 0))
    mat = lambda m, n: pl.BlockSpec((m, n), lambda b, s: (0, 0))
    out = pl.BlockSpec((tb, WIDTH), lambda b, s: (row(b, s), 0))
    kd = GLA_HEADS * GLA_DK
    carry = lambda n: pltpu.VMEM((SUBLANE, n), F32)
    y_shape = jax.ShapeDtypeStruct((bsz * seq, WIDTH), BF16)
    return pl.pallas_call(
        _mixers_kernel,
        grid=(bsz, spt),
        in_specs=[seg("rw_r"), seg("rw_k"), seg("rw_v"), seg("rw_gate"), low("rw_wa"),
                  seg("m2_x"), seg("m2_bc"), seg("m2_z"), low("m2_dt"),
                  seg("gla_qk"), seg("gla_v"), seg("gla_gate"), low("gla_gk"),
                  vec(WIDTH), vec(WIDTH), vec(WIDTH), vec(LANE), vec(WIDTH), mat(LANE, WIDTH),
                  vec(WIDTH), mat(LANE, WIDTH), vec(WIDTH), vec(WIDTH), vec(WIDTH), vec(WIDTH),
                  vec(WIDTH), mat(MXU_DIM, MXU_DIM),
                  mat(M2_CONV, WIDTH), mat(M2_CONV, WIDTH), vec(WIDTH), vec(WIDTH),
                  vec(LANE), vec(LANE), vec(WIDTH), vec(WIDTH), mat(LANE, WIDTH),
                  mat(LANE, kd), vec(kd), vec(WIDTH)],
        out_specs=[out, out, out],
        out_shape=[y_shape, y_shape, y_shape],
        scratch_shapes=[carry(WIDTH), carry(WIDTH), carry(WIDTH), carry(LANE),
                        pltpu.VMEM((RW_HEADS // 2, 2 * RW_DIM, 2 * RW_DIM), F32),
                        carry(WIDTH), carry(WIDTH),
                        pltpu.VMEM((M2_GROUPS, M2_STATE, WIDTH // M2_GROUPS), F32),
                        pltpu.VMEM((tb, WIDTH), F32),
                        pltpu.VMEM((GLA_HEADS, GLA_DV, GLA_DK), F32)],
        compiler_params=pltpu.CompilerParams(dimension_semantics=("arbitrary", "arbitrary")),
        name="mixers",
    )(*([proj] * 13),
      p["mu_r"], p["mu_k"], p["mu_v"], p["mu_wa"], p["w0"], p["w_up"], p["a0"], p["a_up"],
      p["k_k"], p["k_a"], p["r_k"], p["ln_g"], p["ln_b"], p["ones_bd"],
      p["cw_x"], p["cw_bc"], p["cb_x"], p["cb_bc"], p["dt_bias"], p["a_log"], p["d_skip"],
      p["m2_norm_w"], p["expand"],
      p["gk_up"], p["gk_b"], p["gla_norm_w"])


def _merge_kernel(alpha, y_rw, y_m2, y_gla, lg0, lg1, lg2, x_ref, mod_ref, wb, w_out, pg, pb,
                  o_ref):
    gate = lambda lg: _sigmoid(lg[...].astype(F32))
    merged = (gate(lg0) * _bdot(y_rw[...], wb[0])
              + gate(lg1) * _bdot(y_m2[...], wb[1])
              + gate(lg2) * _bdot(y_gla[...], wb[2]))
    y = _bdot(merged, w_out[...])
    res = alpha * x_ref[...] + (1.0 + mod_ref[2:3, :]) * y
    mu = jnp.mean(res, axis=-1, keepdims=True)
    rc = res - mu
    var = jnp.mean(rc * rc, axis=-1, keepdims=True)
    o_ref[...] = rc * lax.rsqrt(var + LN_EPS) * pg[...] + pb[...]


def _merge(y_rw, y_m2, y_gla, logits, x2, mod, p, seq, alpha):
    t, d = x2.shape
    tm = min(512, seq)
    tiles_per_seq = seq // tm
    ytile = lambda: pl.BlockSpec((tm, WIDTH), lambda i: (i, 0))
    logit = lambda j: pl.BlockSpec((tm, d), lambda i, j=j: (i, j))
    return pl.pallas_call(
        functools.partial(_merge_kernel, alpha),
        grid=(t // tm,),
        in_specs=[ytile(), ytile(), ytile(), logit(0), logit(1), logit(2),
                  pl.BlockSpec((tm, d), lambda i: (i, 0)),
                  pl.BlockSpec((None, 3, d), lambda i: (i // tiles_per_seq, 0, 0)),
                  pl.BlockSpec((N_BRANCH, WIDTH, d), lambda i: (0, 0, 0)),
                  pl.BlockSpec((d, d), lambda i: (0, 0)),
                  pl.BlockSpec((1, d), lambda i: (0, 0)),
                  pl.BlockSpec((1, d), lambda i: (0, 0))],
        out_specs=pl.BlockSpec((tm, d), lambda i: (i, 0)),
        out_shape=jax.ShapeDtypeStruct((t, d), F32),
        compiler_params=pltpu.CompilerParams(dimension_semantics=("parallel",)),
        name="merge_out",
    )(y_rw, y_m2, y_gla, logits, logits, logits, x2, mod, p["w_branch"], p["w_out"], p["post_g"],
      p["post_b"])


def _pad_cols(a, n):
    return jnp.pad(a, ((0, 0), (0, n - a.shape[1])))


def _pad_rows_at(a, start, n):
    return jnp.pad(a, ((start, n - start - a.shape[0]), (0, 0)))


def _layer_params(l, w_in, rw_mu, rw_w0, rw_w_up, rw_a0, rw_a_up, rw_k_k, rw_k_a, rw_r_k,
                  rw_ln_g, rw_ln_b, m2_conv_w, m2_conv_b, m2_dt_bias, m2_a_log, m2_d_skip,
                  m2_norm_w, gla_gk_up, gla_gk_b, gla_norm_w, w_branch, w_out, post_g, post_b):
    W = WIDTH
    rank2 = 2 * RW_RANK
    sizes = (3 * W + rank2, W, W + 2 * M2_GROUPS * M2_STATE, M2_HEADS, W,
             GLA_HEADS * GLA_DK, GLA_HEADS * GLA_DK, GLA_HEADS * GLA_DV, GLA_RANK, W,
             N_BRANCH * D_MODEL)
    offs = [0]
    for s in sizes:
        offs.append(offs[-1] + s)
    (o_rw, o_rwg, o_xbc, o_dt, o_z, o_q, o_k, o_v, o_gk, o_gg, o_merge, _) = offs
    wl = w_in[l]
    cols = lambda a, n: wl[:, a:a + n]
    segs = {
        "rw_r": cols(o_rw, W), "rw_k": cols(o_rw + W, W), "rw_v": cols(o_rw + 2 * W, W),
        "rw_gate": cols(o_rwg, W), "m2_x": cols(o_xbc, W), "m2_bc": cols(o_xbc + W, W),
        "m2_z": cols(o_z, W), "gla_qk": cols(o_q, W), "gla_v": cols(o_v, W),
        "gla_gate": cols(o_gg, W),
        "rw_wa": cols(o_rw + 3 * W, rank2), "m2_dt": _pad_cols(cols(o_dt, M2_HEADS), LANE),
        "gla_gk": _pad_cols(cols(o_gk, GLA_RANK), LANE),
    }
    w_logit = cols(o_merge, LOGIT_WIDTH).astype(BF16)
    w_mixin = jnp.concatenate([segs[n] for n in SEG512] + [segs[n] for n in SEG128], axis=1)
    w_mixin = _pad_cols(w_mixin, MIXIN_WIDTH).astype(BF16)

    row = lambda a: a.reshape(1, -1)
    mu = rw_mu[l]
    head_of_ch = jnp.arange(MXU_DIM) // RW_DIM
    ones_bd = (head_of_ch[:, None] == head_of_ch[None, :]).astype(BF16)
    expand = (jnp.arange(LANE)[:, None] == (jnp.arange(W) // M2_DIM)[None, :]).astype(BF16)
    cw = m2_conv_w[l]
    cb = m2_conv_b[l]
    return {
        "w_logit": w_logit, "w_mixin": w_mixin,
        "mu_r": row(mu[:W]), "mu_k": row(mu[W:2 * W]), "mu_v": row(mu[2 * W:3 * W]),
        "mu_wa": row(mu[3 * W:]),
        "w0": row(rw_w0[l]), "w_up": _pad_rows_at(rw_w_up[l], 0, LANE),
        "a0": row(rw_a0[l]), "a_up": _pad_rows_at(rw_a_up[l], RW_RANK, LANE),
        "k_k": row(rw_k_k[l]), "k_a": row(rw_k_a[l]), "r_k": row(rw_r_k[l]),
        "ln_g": row(rw_ln_g[l]), "ln_b": row(rw_ln_b[l]), "ones_bd": ones_bd,
        "cw_x": cw[:, :W], "cw_bc": cw[:, W:], "cb_x": row(cb[:W]), "cb_bc": row(cb[W:]),
        "dt_bias": _pad_cols(row(m2_dt_bias[l]), LANE), "a_log": _pad_cols(row(m2_a_log[l]), LANE),
        "d_skip": row(jnp.repeat(m2_d_skip[l], M2_DIM)), "m2_norm_w": row(m2_norm_w[l]),
        "expand": expand,
        "gk_up": _pad_rows_at(gla_gk_up[l], 0, LANE), "gk_b": row(gla_gk_b[l]),
        "gla_norm_w": row(jnp.tile(gla_norm_w[l], GLA_HEADS)),
        "w_branch": w_branch[l].astype(BF16), "w_out": w_out[l].astype(BF16),
        "post_g": row(post_g[l]), "post_b": row(post_b[l]),
    }


def kernel(x, c, ada_w, ada_b, w_in, rw_mu, rw_w0, rw_w_up, rw_a0, rw_a_up, rw_k_k, rw_k_a, rw_r_k, rw_ln_g, rw_ln_b, m2_conv_w, m2_conv_b, m2_dt_bias, m2_a_log, m2_d_skip, m2_norm_w, gla_gk_up, gla_gk_b, gla_norm_w, w_branch, w_out, post_g, post_b):
    bsz, seq, d = x.shape
    depth = ada_w.shape[0]
    alpha = (2.0 * depth) ** 0.25
    mod_all = _adaln(c, ada_w, ada_b).reshape(depth, bsz, 3, d)
    x2 = x.reshape(bsz * seq, d)
    for l in range(depth):
        p = _layer_params(l, w_in, rw_mu, rw_w0, rw_w_up, rw_a0, rw_a_up, rw_k_k, rw_k_a,
                          rw_r_k, rw_ln_g, rw_ln_b, m2_conv_w, m2_conv_b, m2_dt_bias, m2_a_log,
                          m2_d_skip, m2_norm_w, gla_gk_up, gla_gk_b, gla_norm_w, w_branch, w_out,
                          post_g, post_b)
        mod = mod_all[l]
        logits, mixin = _inproj(x2, mod, p["w_logit"], p["w_mixin"], seq)
        y_rw, y_m2, y_gla = _mixers(mixin, p, bsz, seq)
        x2 = _merge(y_rw, y_m2, y_gla, logits, x2, mod, p, seq, alpha)
    return x2.reshape(bsz, seq, d)
```

```python
import functools

import jax
import jax.numpy as jnp
from jax import lax
from jax.experimental import pallas as pl
from jax.experimental.pallas import tpu as pltpu

F32 = jnp.float32
BF16 = jnp.bfloat16
HIGHEST = lax.Precision.HIGHEST

D_MODEL = 1024
WIDTH = 512
LN_EPS = 1e-5
RW_HEADS, RW_DIM, RW_RANK = 8, 64, 64
RW_GN_EPS = 64e-5
RW_CHUNK = 64
RW_SUBCHUNKS = 4
RW_BLOCK = 2 * RW_CHUNK
M2_HEADS, M2_DIM, M2_GROUPS, M2_STATE, M2_CONV = 8, 64, 2, 128, 4
M2_CHUNK = 128
M2_EPS = 1e-5
GLA_HEADS, GLA_DK, GLA_DV, GLA_RANK = 4, 64, 128, 16
GLA_TAU = 16.0
GLA_CHUNK = 64
GLA_SUBCHUNKS = 2
GLA_EPS = 1e-5
N_BRANCH = 3
MIX_TILE = 512

LANE = 128
SUBLANE = 8
MXU_DIM = 256

SEG512 = ("rw_r", "rw_k", "rw_v", "rw_gate", "m2_x", "m2_bc", "m2_z", "gla_qk", "gla_v", "gla_gate")
SEG128 = ("rw_wa", "m2_dt", "gla_gk")
COL128_BASE = WIDTH * len(SEG512)
LOGIT_WIDTH = N_BRANCH * D_MODEL
PROJ_COL_STEPS = 4
LOGIT_TILE_N = LOGIT_WIDTH // PROJ_COL_STEPS
MIXIN_WIDTH = -(-(COL128_BASE + LANE * len(SEG128)) // (LANE * PROJ_COL_STEPS)) * LANE * PROJ_COL_STEPS
MIXIN_TILE_N = MIXIN_WIDTH // PROJ_COL_STEPS
assert LOGIT_TILE_N * PROJ_COL_STEPS == LOGIT_WIDTH and LOGIT_TILE_N % LANE == 0

P_M2 = 1
P_GLA = 1
P_LOWRANK = 3
HEAD_SUM_PIECES = 1

NN = ((1,), (0,))
NT = ((1,), (1,))
TN = ((0,), (0,))


def _blk512(name):
    return SEG512.index(name)


def _blk128(name):
    return (COL128_BASE + LANE * SEG128.index(name)) // LANE


def _bdot(a, b):
    return jnp.dot(a.astype(BF16), b.astype(BF16), preferred_element_type=F32)


def _dg(a, b, dims):
    return lax.dot_general(a, b, (dims, ((), ())), preferred_element_type=F32)


def _split2(x):
    hi = x.astype(BF16)
    return hi, (x - hi.astype(F32)).astype(BF16)


def _dot(a, b, dims=NN, passes=1):
    if passes == 1:
        return _dg(a.astype(BF16), b.astype(BF16), dims)
    a_hi, a_lo = _split2(a)
    b_hi, b_lo = _split2(b)
    return _dg(a_hi, b_hi, dims) + _dg(a_lo, b_hi, dims) + _dg(a_hi, b_lo, dims)


def _dot_exact_rhs(a, b_bf16, pieces=2):
    out = None
    for _ in range(pieces):
        piece = a.astype(BF16)
        a = a - piece.astype(F32)
        term = _dg(piece, b_bf16, NN)
        out = term if out is None else out + term
    return out


def _chunk_cumsum(x, chunk):
    pos = lax.broadcasted_iota(jnp.int32, x.shape, 0) & (chunk - 1)
    step = 1
    while step < chunk:
        x = x + jnp.where(pos >= step, pltpu.roll(x, step, 0), 0.0)
        step *= 2
    return x


def _sigmoid(x):
    return jax.nn.sigmoid(x)


def _silu(x):
    return x * jax.nn.sigmoid(x)


def _softplus(x):
    return jnp.maximum(x, 0.0) + jnp.log1p(jnp.exp(-jnp.abs(x)))


def _tri(n):
    return (lax.broadcasted_iota(jnp.int32, (n, n), 0)
            >= lax.broadcasted_iota(jnp.int32, (n, n), 1))


def _shift_rows(x, prev, j):
    ext = jnp.concatenate([prev, x], axis=0)
    return pltpu.roll(ext, j, 0)[SUBLANE:, :]


def _emit_interleaved(stages, fillers):
    n_s, n_f = len(stages), len(fillers)
    after = [((k + 1) * n_s) // (n_f + 1) for k in range(n_f)]
    k = 0
    for i, stage in enumerate(stages):
        stage()
        while k < n_f and after[k] <= i + 1:
            fillers[k]()
            k += 1
    for filler in fillers[k:]:
        filler()


def _adaln_kernel(c_ref, w_ref, b_ref, o_ref):
    o_ref[...] = lax.dot_general(_silu(c_ref[...]), w_ref[...], (NN, ((), ())), precision=HIGHEST,
                                 preferred_element_type=F32) + b_ref[...]


def _adaln(c, ada_w, ada_b):
    depth, d, n = ada_w.shape
    bsz = c.shape[0]
    tn = 768
    return pl.pallas_call(
        _adaln_kernel,
        grid=(depth, n // tn),
        in_specs=[pl.BlockSpec((bsz, d), lambda l, j: (0, 0)),
                  pl.BlockSpec((None, d, tn), lambda l, j: (l, 0, j)),
                  pl.BlockSpec((None, 1, tn), lambda l, j: (l, 0, j))],
        out_specs=pl.BlockSpec((None, bsz, tn), lambda l, j: (l, 0, j)),
        out_shape=jax.ShapeDtypeStruct((depth, bsz, n), F32),
        name="adaln_mod",
    )(c, ada_w, ada_b.reshape(depth, 1, n))


def _inproj_kernel(x_ref, mod_ref, wl_ref, wm_ref, logit_ref, mixin_ref, h_scr):
    @pl.when(pl.program_id(1) == 0)
    def _():
        x = x_ref[...]
        mu = jnp.mean(x, axis=-1, keepdims=True)
        xc = x - mu
        var = jnp.mean(xc * xc, axis=-1, keepdims=True)
        h = xc * lax.rsqrt(var + LN_EPS) * (1.0 + mod_ref[1:2, :]) + mod_ref[0:1, :]
        h_scr[...] = h.astype(BF16)

    h = h_scr[...]
    logit_ref[...] = jnp.dot(h, wl_ref[...], preferred_element_type=F32).astype(logit_ref.dtype)
    mixin_ref[...] = jnp.dot(h, wm_ref[...], preferred_element_type=F32)


def _inproj(x2, mod, w_logit, w_mixin, seq):
    t, d = x2.shape
    tm = min(1024, seq)
    tiles_per_seq = seq // tm
    return pl.pallas_call(
        _inproj_kernel,
        grid=(t // tm, PROJ_COL_STEPS),
        in_specs=[pl.BlockSpec((tm, d), lambda i, j: (i, 0)),
                  pl.BlockSpec((None, 3, d), lambda i, j: (i // tiles_per_seq, 0, 0)),
                  pl.BlockSpec((d, LOGIT_TILE_N), lambda i, j: (0, j)),
                  pl.BlockSpec((d, MIXIN_TILE_N), lambda i, j: (0, j))],
        out_specs=[pl.BlockSpec((tm, LOGIT_TILE_N), lambda i, j: (i, j)),
                   pl.BlockSpec((tm, MIXIN_TILE_N), lambda i, j: (i, j))],
        out_shape=[jax.ShapeDtypeStruct((t, LOGIT_WIDTH), BF16),
                   jax.ShapeDtypeStruct((t, MIXIN_WIDTH), F32)],
        scratch_shapes=[pltpu.VMEM((tm, d), BF16)],
        compiler_params=pltpu.CompilerParams(dimension_semantics=("parallel", "arbitrary")),
        name="inproj",
    )(x2, mod, w_logit, w_mixin)


def _rwkv_plan(tb, r_ref, k_ref, v_ref, g_ref, wa_ref, mu_r, mu_k, mu_v, mu_wa, w0, w_up, a0, a_up,
               k_k, k_a, r_k, ln_g, ln_b, ones_bd, o_ref, car_r, car_k, car_v, car_wa, st):
    L, N, RB = RW_CHUNK, RW_DIM, RW_BLOCK
    nchunk, nblock = tb // L, tb // RB
    blocks = [dict() for _ in range(nblock)]
    ys = [[None] * (RW_HEADS // 2) for _ in range(nchunk)]

    def head_sum(t):
        parts = [_dot_exact_rhs(t[:, c:c + MXU_DIM], ones_bd[...], HEAD_SUM_PIECES)
                 for c in range(0, WIDTH, MXU_DIM)]
        return jnp.concatenate(parts, axis=1)

    def prepare(bi):
        d = blocks[bi]
        rows = slice(bi * RB, (bi + 1) * RB)

        def lerp(x_ref, car, mu):
            x = x_ref[rows, :]
            prev = car[...] if bi == 0 else x_ref[bi * RB - SUBLANE:bi * RB, :]
            if bi == nblock - 1:
                car[...] = x[RB - SUBLANE:, :]
            return x + mu[...] * (_shift_rows(x, prev, 1) - x)

        def p_rk():
            d["r"] = lerp(r_ref, car_r, mu_r)
            d["k"] = lerp(k_ref, car_k, mu_k)

        def p_v():
            d["v"] = lerp(v_ref, car_v, mu_v)
            d["wa"] = lerp(wa_ref, car_wa, mu_wa)

        def p_decay():
            w = w0[...] + _dot(jnp.tanh(d["wa"]), w_up[...], NN, P_LOWRANK)
            d["lw"] = -0.6065306597126334 * _sigmoid(w)
            d["a"] = _sigmoid(a0[...] + _dot(d["wa"], a_up[...], NN, P_LOWRANK))

        def p_kk():
            kk = d["k"] * k_k[...]
            d["kk"] = kk * lax.rsqrt(jnp.maximum(head_sum(kk * kk), 1e-24))

        def p_k2():
            d["k2"] = d["k"] * (1.0 + (d["a"] - 1.0) * k_a[...])
            d["bonus"] = head_sum(d["r"] * d["k2"] * r_k[...]) * d["v"]
            d["b"] = d["kk"] * d["a"]

        def p_cumsum():
            d["g"] = _chunk_cumsum(d["lw"], L)

        def p_scale():
            g = d["g"]
            eng = jnp.exp(-g)
            d["at"] = -d["kk"] * jnp.exp(g - d["lw"])
            d["bt"] = d["b"] * eng
            d["kt"] = d["k2"] * eng
            d["rt"] = d["r"] * jnp.exp(g)

        return [p_rk, p_v, p_decay, p_kk, p_k2, p_cumsum, p_scale]

    P2 = 2 * L
    npair = RW_HEADS // 2
    lane_lo = lax.broadcasted_iota(jnp.int32, (L, P2), 1) < N
    zero_b = jnp.zeros((L, P2), BF16)

    def hat(xb):
        return jnp.concatenate([jnp.where(lane_lo, xb, zero_b), jnp.where(lane_lo, zero_b, xb)],
                               axis=0)

    bf = lambda x: x.astype(BF16)
    ri = lax.broadcasted_iota(jnp.int32, (P2, 2 * P2), 0)
    ci = lax.broadcasted_iota(jnp.int32, (P2, 2 * P2), 1)
    tpos = ri & (L - 1)
    spos = ci & (L - 1)
    mask_ar = jnp.where(ri < L, tpos, tpos + 1) > spos
    eye_side = (lax.broadcasted_iota(jnp.int32, (L, P2), 0)
                == (lax.broadcasted_iota(jnp.int32, (L, P2), 1) & (L - 1))).astype(F32)
    eye = (lax.broadcasted_iota(jnp.int32, (P2, P2), 0)
           == lax.broadcasted_iota(jnp.int32, (P2, P2), 1)).astype(F32)

    def group(chunks):
        probs = [dict(i=i, j=j) for i in chunks for j in range(npair)]

        def load(i):
            d = blocks[(i * L) // RB]
            rows = slice((i * L) % RB, (i * L) % RB + L)
            gc = d["g"][rows]
            gl = gc[L - 1:L, :]
            egl = jnp.exp(gl - gc)
            dec = jnp.exp(gl)
            bh = d["b"][rows] * egl
            kh = d["k2"][rows] * egl
            for q in probs:
                if q["i"] != i:
                    continue
                ls = slice(q["j"] * P2, (q["j"] + 1) * P2)
                q.update(dec=dec[:, ls], r=d["rt"][rows, ls], rb=bf(d["rt"][rows, ls]),
                         ab=bf(d["at"][rows, ls]), ha=hat(bf(d["at"][rows, ls])),
                         hb=hat(bf(d["bt"][rows, ls])), hk=hat(bf(d["kt"][rows, ls])),
                         hv=hat(bf(d["v"][rows, ls])), hbh=hat(bf(bh[:, ls])),
                         hkh=hat(bf(kh[:, ls])))

        def interactions(q):
            m = _dg(jnp.concatenate([q["ab"], q["rb"]], axis=0),
                    jnp.concatenate([q["hb"], q["hk"]], axis=0), NT)
            m = jnp.where(mask_ar, m, 0.0)
            q["a_ak"] = bf(m[:L, P2:])
            q["a_r"] = bf(m[L:, :])
            q["p"] = m[:L, :P2]
            q["t"] = eye_side + q["p"]

        def square(q):
            pb = bf(q["p"])
            q["p"] = _dg(pb, hat(pb), NN)

        def double(q):
            pb = bf(q["p"])
            tp = _dg(pb, jnp.concatenate([hat(bf(q["t"])), hat(pb)], axis=1), NN)
            q["t"], q["p"] = q["t"] + tp[:, :P2], tp[:, P2:]

        def inverse_done(q):
            q["t"] = q["t"] + _dg(bf(q["p"]), hat(bf(q["t"])), NN)

        def av(q):
            q["av"] = _dg(q["a_ak"], q["hv"], NN)

        def tu(q):
            out = _dg(bf(q["t"]), jnp.concatenate([q["ha"], hat(bf(q["av"]))], axis=1), NN)
            q["hwm"], q["hu0"] = hat(bf(out[:, :P2])), hat(bf(out[:, P2:]))

        def ru(q):
            rhs = jnp.concatenate(
                [jnp.concatenate([q["hwm"], q["hu0"]], axis=1),
                 jnp.concatenate([jnp.zeros((P2, P2), BF16), q["hv"]], axis=1)], axis=0)
            out = _dg(q["a_r"], rhs, NN)
            q["rm"] = bf(q["r"] + out[:, :P2])
            q["y0"] = out[:, P2:]

        def maps(q):
            q["mt"] = bf(eye * q["dec"] + _dg(q["hwm"], q["hbh"], TN))
            q["cc"] = _dg(jnp.concatenate([q["hu0"], q["hv"]], axis=0),
                          jnp.concatenate([q["hbh"], q["hkh"]], axis=0), TN)

        def recur(q):
            s0 = bf(st[q["j"]])
            ys[q["i"]][q["j"]] = _dg(q["rm"], s0, NT) + q["y0"]
            st[q["j"]] = _dg(s0, q["mt"], NN) + q["cc"]

        pieces = [functools.partial(load, i) for i in chunks]
        for stage in (interactions, square, double, double, double, double, inverse_done,
                      av, tu, ru, maps, recur):
            pieces += [functools.partial(stage, q) for q in probs]
        return pieces

    def close(bi):
        d = blocks[bi]
        rows = slice(bi * RB, (bi + 1) * RB)

        def c_mean():
            y = jnp.concatenate([jnp.concatenate(ys[i], axis=1)
                                 for i in range(bi * RB // L, (bi + 1) * RB // L)], axis=0)
            d["yc"] = y - head_sum(y) * (1.0 / N)

        def c_var():
            d["var"] = head_sum(d["yc"] * d["yc"]) * (1.0 / N)

        def c_out():
            yn = d["yc"] * lax.rsqrt(d["var"] + RW_GN_EPS) * ln_g[...] + ln_b[...]
            o_ref[rows, :] = ((yn + d["bonus"]) * _silu(g_ref[rows, :])).astype(o_ref.dtype)

        return [c_mean, c_var, c_out]

    groups = [group(range(c0, c0 + RW_SUBCHUNKS)) for c0 in range(0, nchunk, RW_SUBCHUNKS)]
    return ([prepare(bi) for bi in range(nblock)], groups, [close(bi) for bi in range(nblock)])


def _mamba_pieces(tb, x_ref, bc_ref, z_ref, dt_ref, cw_x, cw_bc, cb_x, cb_bc, dt_bias, a_log,
                  d_skip, norm_w, expand, o_ref, car_x, car_bc, st, y_scr):
    L = M2_CHUNK
    nchunk = tb // L
    gw = WIDTH // M2_GROUPS
    hpg = M2_HEADS // M2_GROUPS
    tri_incl = _tri(L)

    def chunk(c):
        d = {}
        rows = slice(c * L, (c + 1) * L)

        def conv(ref, car, w, bias):
            x = ref[rows, :]
            prev = car[...] if c == 0 else ref[c * L - SUBLANE:c * L, :]
            y = x * w[M2_CONV - 1:M2_CONV, :] + bias[...]
            for j in range(1, M2_CONV):
                y = y + _shift_rows(x, prev, j) * w[M2_CONV - 1 - j:M2_CONV - j, :]
            if c == nchunk - 1:
                car[...] = x[L - SUBLANE:, :]
            return _silu(y)

        def m_conv_x():
            d["xs"] = conv(x_ref, car_x, cw_x, cb_x)

        def m_conv_bc():
            bcc = conv(bc_ref, car_bc, cw_bc, cb_bc)
            d["bm"] = [bcc[:, g * M2_STATE:(g + 1) * M2_STATE] for g in range(M2_GROUPS)]
            d["cm"] = [bcc[:, gw + g * M2_STATE:gw + (g + 1) * M2_STATE]
                       for g in range(M2_GROUPS)]

        def m_decay():
            ex = expand[...]
            dt = _softplus(dt_ref[rows, :] + dt_bias[...])
            a_c = _chunk_cumsum(dt * (-jnp.exp(a_log[...])), L)
            d["a_c"], d["a_t"] = a_c, a_c.T
            a_x = _dot_exact_rhs(a_c, ex, 3)
            d["xdt"] = d["xs"] * _dot_exact_rhs(dt, ex, 3)
            a_last = a_x[L - 1:L, :]
            d["xdt_end"] = d["xdt"] * jnp.exp(a_last - a_x)
            d["e_ax"] = jnp.exp(a_x)
            d["e_last"] = jnp.exp(a_last)

        def m_cb():
            d["cb"] = [_dot(d["cm"][g], d["bm"][g], NT, P_M2) for g in range(M2_GROUPS)]

        def m_head(h):
            diff = d["a_c"][:, h:h + 1] - d["a_t"][h:h + 1, :]
            dec = jnp.exp(jnp.where(tri_incl, diff, -jnp.inf))
            d["yd%d" % h] = _dot(d["cb"][h // hpg] * dec,
                                 d["xdt"][:, h * M2_DIM:(h + 1) * M2_DIM], NN, P_M2)

        def m_state(g):
            gs = slice(g * gw, (g + 1) * gw)
            kv = _dot(d["bm"][g], d["xdt_end"][:, gs], TN, P_M2)
            s0 = st[g]
            y_off = _dot(d["cm"][g], s0, NN, P_M2) * d["e_ax"][:, gs]
            st[g] = s0 * d["e_last"][:, gs] + kv
            for hh in range(hpg):
                h = g * hpg + hh
                y_scr[rows, h * M2_DIM:(h + 1) * M2_DIM] = (
                    d["yd%d" % h] + y_off[:, hh * M2_DIM:(hh + 1) * M2_DIM])

        def m_out():
            y = (y_scr[rows, :] + d["xs"] * d_skip[...]) * _silu(z_ref[rows, :])
            outs = []
            for g in range(M2_GROUPS):
                yg = y[:, g * gw:(g + 1) * gw]
                ms = jnp.mean(yg * yg, axis=-1, keepdims=True)
                outs.append(yg * lax.rsqrt(ms + M2_EPS))
            o_ref[rows, :] = (jnp.concatenate(outs, axis=-1) * norm_w[...]).astype(o_ref.dtype)

        return ([m_conv_x, m_conv_bc, m_decay, m_cb]
                + [functools.partial(m_head, h) for h in range(M2_HEADS)]
                + [functools.partial(m_state, g) for g in range(M2_GROUPS)] + [m_out])

    return [chunk(c) for c in range(nchunk)]


def _gla_pieces(tb, qk_ref, v_ref, g_ref, gk_ref, gk_up, gk_b, norm_w, o_ref, st):
    L = GLA_CHUNK
    span = L * GLA_SUBCHUNKS
    kd = GLA_HEADS * GLA_DK
    tri_incl = _tri(L)

    def block(i):
        d = {}
        rows = slice(i * span, (i + 1) * span)
        probs = [dict(sub=sub, h=h) for sub in range(GLA_SUBCHUNKS) for h in range(GLA_HEADS)]

        def g_gates():
            la = -_softplus(-(_dot(gk_ref[rows, :], gk_up[...], NN, P_LOWRANK) + gk_b[...]))
            g = _chunk_cumsum(la * (1.0 / GLA_TAU), L)
            qk = qk_ref[rows, :]
            d["g"], d["kraw"] = g, qk[:, kd:]
            d["qg"] = qk[:, :kd] * (GLA_DK ** -0.5) * jnp.exp(g)
            d["kg"] = d["kraw"] * jnp.exp(-g)

        def g_split():
            vv = v_ref[rows, :]
            for sub in range(GLA_SUBCHUNKS):
                rs = slice(sub * L, (sub + 1) * L)
                gc = d["g"][rs]
                gl = gc[L - 1:L, :]
                kh = d["kraw"][rs] * jnp.exp(gl - gc)
                dec = jnp.exp(gl)
                for q in probs:
                    if q["sub"] != sub:
                        continue
                    ks = slice(q["h"] * GLA_DK, (q["h"] + 1) * GLA_DK)
                    vs = slice(q["h"] * GLA_DV, (q["h"] + 1) * GLA_DV)
                    q.update(qg=d["qg"][rs, ks], kg=d["kg"][rs, ks], v=vv[rs, vs], kh=kh[:, ks],
                             dec=dec[:, ks])

        def g_att(sub):
            for q in probs:
                if q["sub"] == sub:
                    q["att"] = jnp.where(tri_incl, _dot(q["qg"], q["kg"], NT, P_GLA), 0.0)

        def g_av(sub):
            for q in probs:
                if q["sub"] == sub:
                    q["o"] = _dot(q["att"], q["v"], NN, P_GLA)
                    q["kv"] = _dot(q["v"], q["kh"], TN, P_GLA)

        def g_recur(sub):
            for q in probs:
                if q["sub"] == sub:
                    s0 = st[q["h"]]
                    oh = q["o"] + _dot(q["qg"], s0, NT, P_GLA)
                    st[q["h"]] = s0 * q["dec"] + q["kv"]
                    ms = jnp.mean(oh * oh, axis=-1, keepdims=True)
                    q["on"] = oh * lax.rsqrt(ms + GLA_EPS)

        def g_out():
            o = jnp.concatenate(
                [jnp.concatenate([q["on"] for q in probs if q["sub"] == sub], axis=1)
                 for sub in range(GLA_SUBCHUNKS)], axis=0)
            o_ref[rows, :] = (o * norm_w[...] * _silu(g_ref[rows, :])).astype(o_ref.dtype)

        subs = range(GLA_SUBCHUNKS)
        return ([g_gates, g_split] + [functools.partial(g_att, s) for s in subs]
                + [functools.partial(g_av, s) for s in subs]
                + [functools.partial(g_recur, s) for s in subs] + [g_out])

    return [block(i) for i in range(tb // span)]


def _mixers_kernel(r_ref, k_ref, v_ref, rg_ref, wa_ref, mx_ref, mbc_ref, mz_ref, mdt_ref,
                   gqk_ref, gv_ref, gg_ref, ggk_ref,
                   mu_r, mu_k, mu_v, mu_wa, w0, w_up, a0, a_up, k_k, k_a, r_k, ln_g, ln_b, ones_bd,
                   cw_x, cw_bc, cb_x, cb_bc, dt_bias, a_log, d_skip, m2_norm_w, expand,
                   gk_up, gk_b, gla_norm_w,
                   o_rw, o_m2, o_gla,
                   car_r, car_k, car_v, car_wa, st_rw, car_x, car_bc, st_m2, y_m2, st_gla):
    tb = r_ref.shape[0]

    @pl.when(pl.program_id(1) == 0)
    def _():
        for ref in (car_r, car_k, car_v, car_wa, st_rw, car_x, car_bc, st_m2, st_gla):
            ref[...] = jnp.zeros_like(ref)

    prepare, groups, close = _rwkv_plan(
        tb, r_ref, k_ref, v_ref, rg_ref, wa_ref, mu_r, mu_k, mu_v, mu_wa, w0, w_up, a0, a_up,
        k_k, k_a, r_k, ln_g, ln_b, ones_bd, o_rw, car_r, car_k, car_v, car_wa, st_rw)
    m2 = _mamba_pieces(tb, mx_ref, mbc_ref, mz_ref, mdt_ref, cw_x, cw_bc, cb_x, cb_bc, dt_bias,
                       a_log, d_skip, m2_norm_w, expand, o_m2, car_x, car_bc, st_m2, y_m2)
    gla = _gla_pieces(tb, gqk_ref, gv_ref, gg_ref, ggk_ref, gk_up, gk_b, gla_norm_w, o_gla,
                      st_gla)

    ngroup = len(groups)
    bpg = len(prepare) // ngroup
    share = lambda items, n: items[n * len(items) // ngroup:(n + 1) * len(items) // ngroup]
    flat = lambda lists: [piece for pieces in lists for piece in pieces]
    for piece in flat(prepare[:bpg]):
        piece()
    for n in range(ngroup):
        fillers = flat(prepare[(n + 1) * bpg:(n + 2) * bpg])
        if n > 0:
            fillers += flat(close[(n - 1) * bpg:n * bpg])
        side = [flat(share(m2, n)), flat(share(gla, n))]
        fillers += [piece for k in range(max(map(len, side))) for s in side for piece in s[k:k + 1]]
        _emit_interleaved(groups[n], fillers)
    for piece in flat(close[(ngroup - 1) * bpg:]):
        piece()


def _mixers(proj, p, bsz, seq):
    tb = min(MIX_TILE, seq)
    assert tb % (RW_SUBCHUNKS * RW_CHUNK) == 0 and tb % M2_CHUNK == 0
    spt = seq // tb
    row = lambda b, s: b * spt + s
    seg = lambda name: pl.BlockSpec((tb, WIDTH), lambda b, s, j=_blk512(name): (row(b, s), j))
    low = lambda name: pl.BlockSpec((tb, LANE), lambda b, s, j=_blk128(name): (row(b, s), j))
    vec = lambda n: pl.BlockSpec((1, n), lambda b, s: (0, 0))
    mat = lambda m, n: pl.BlockSpec((m, n), lambda b, s: (0, 0))
    out = pl.BlockSpec((tb, WIDTH), lambda b, s: (row(b, s), 0))
    kd = GLA_HEADS * GLA_DK
    carry = lambda n: pltpu.VMEM((SUBLANE, n), F32)
    y_shape = jax.ShapeDtypeStruct((bsz * seq, WIDTH), BF16)
    return pl.pallas_call(
        _mixers_kernel,
        grid=(bsz, spt),
        in_specs=[seg("rw_r"), seg("rw_k"), seg("rw_v"), seg("rw_gate"), low("rw_wa"),
                  seg("m2_x"), seg("m2_bc"), seg("m2_z"), low("m2_dt"),
                  seg("gla_qk"), seg("gla_v"), seg("gla_gate"), low("gla_gk"),
                  vec(WIDTH), vec(WIDTH), vec(WIDTH), vec(LANE), vec(WIDTH), mat(LANE, WIDTH),
                  vec(WIDTH), mat(LANE, WIDTH), vec(WIDTH), vec(WIDTH), vec(WIDTH), vec(WIDTH),
                  vec(WIDTH), mat(MXU_DIM, MXU_DIM),
                  mat(M2_CONV, WIDTH), mat(M2_CONV, WIDTH), vec(WIDTH), vec(WIDTH),
                  vec(LANE), vec(LANE), vec(WIDTH), vec(WIDTH), mat(LANE, WIDTH),
                  mat(LANE, kd), vec(kd), vec(WIDTH)],
        out_specs=[out, out, out],
        out_shape=[y_shape, y_shape, y_shape],
        scratch_shapes=[carry(WIDTH), carry(WIDTH), carry(WIDTH), carry(LANE),
                        pltpu.VMEM((RW_HEADS // 2, 2 * RW_DIM, 2 * RW_DIM), F32),
                        carry(WIDTH), carry(WIDTH),
                        pltpu.VMEM((M2_GROUPS, M2_STATE, WIDTH // M2_GROUPS), F32),
                        pltpu.VMEM((tb, WIDTH), F32),
                        pltpu.VMEM((GLA_HEADS, GLA_DV, GLA_DK), F32)],
        compiler_params=pltpu.CompilerParams(dimension_semantics=("arbitrary", "arbitrary")),
        name="mixers",
    )(*([proj] * 13),
      p["mu_r"], p["mu_k"], p["mu_v"], p["mu_wa"], p["w0"], p["w_up"], p["a0"], p["a_up"],
      p["k_k"], p["k_a"], p["r_k"], p["ln_g"], p["ln_b"], p["ones_bd"],
      p["cw_x"], p["cw_bc"], p["cb_x"], p["cb_bc"], p["dt_bias"], p["a_log"], p["d_skip"],
      p["m2_norm_w"], p["expand"],
      p["gk_up"], p["gk_b"], p["gla_norm_w"])


def _merge_kernel(alpha, y_rw, y_m2, y_gla, lg0, lg1, lg2, x_ref, mod_ref, wb, w_out, pg, pb,
                  o_ref):
    gate = lambda lg: _sigmoid(lg[...].astype(F32))
    merged = (gate(lg0) * _bdot(y_rw[...], wb[0])
              + gate(lg1) * _bdot(y_m2[...], wb[1])
              + gate(lg2) * _bdot(y_gla[...], wb[2]))
    y = _bdot(merged, w_out[...])
    res = alpha * x_ref[...] + (1.0 + mod_ref[2:3, :]) * y
    mu = jnp.mean(res, axis=-1, keepdims=True)
    rc = res - mu
    var = jnp.mean(rc * rc, axis=-1, keepdims=True)
    o_ref[...] = rc * lax.rsqrt(var + LN_EPS) * pg[...] + pb[...]


def _merge(y_rw, y_m2, y_gla, logits, x2, mod, p, seq, alpha):
    t, d = x2.shape
    tm = min(512, seq)
    tiles_per_seq = seq // tm
    ytile = lambda: pl.BlockSpec((tm, WIDTH), lambda i: (i, 0))
    logit = lambda j: pl.BlockSpec((tm, d), lambda i, j=j: (i, j))
    return pl.pallas_call(
        functools.partial(_merge_kernel, alpha),
        grid=(t // tm,),
        in_specs=[ytile(), ytile(), ytile(), logit(0), logit(1), logit(2),
                  pl.BlockSpec((tm, d), lambda i: (i, 0)),
                  pl.BlockSpec((None, 3, d), lambda i: (i // tiles_per_seq, 0, 0)),
                  pl.BlockSpec((N_BRANCH, WIDTH, d), lambda i: (0, 0, 0)),
                  pl.BlockSpec((d, d), lambda i: (0, 0)),
                  pl.BlockSpec((1, d), lambda i: (0, 0)),
                  pl.BlockSpec((1, d), lambda i: (0, 0))],
        out_specs=pl.BlockSpec((tm, d), lambda i: (i, 0)),
        out_shape=jax.ShapeDtypeStruct((t, d), F32),
        compiler_params=pltpu.CompilerParams(dimension_semantics=("parallel",)),
        name="merge_out",
    )(y_rw, y_m2, y_gla, logits, logits, logits, x2, mod, p["w_branch"], p["w_out"], p["post_g"],
      p["post_b"])


def _pad_cols(a, n):
    return jnp.pad(a, ((0, 0), (0, n - a.shape[1])))


def _pad_rows_at(a, start, n):
    return jnp.pad(a, ((start, n - start - a.shape[0]), (0, 0)))


def _layer_params(l, w_in, rw_mu, rw_w0, rw_w_up, rw_a0, rw_a_up, rw_k_k, rw_k_a, rw_r_k,
                  rw_ln_g, rw_ln_b, m2_conv_w, m2_conv_b, m2_dt_bias, m2_a_log, m2_d_skip,
                  m2_norm_w, gla_gk_up, gla_gk_b, gla_norm_w, w_branch, w_out, post_g, post_b):
    W = WIDTH
    rank2 = 2 * RW_RANK
    sizes = (3 * W + rank2, W, W + 2 * M2_GROUPS * M2_STATE, M2_HEADS, W,
             GLA_HEADS * GLA_DK, GLA_HEADS * GLA_DK, GLA_HEADS * GLA_DV, GLA_RANK, W,
             N_BRANCH * D_MODEL)
    offs = [0]
    for s in sizes:
        offs.append(offs[-1] + s)
    (o_rw, o_rwg, o_xbc, o_dt, o_z, o_q, o_k, o_v, o_gk, o_gg, o_merge, _) = offs
    wl = w_in[l]
    cols = lambda a, n: wl[:, a:a + n]
    segs = {
        "rw_r": cols(o_rw, W), "rw_k": cols(o_rw + W, W), "rw_v": cols(o_rw + 2 * W, W),
        "rw_gate": cols(o_rwg, W), "m2_x": cols(o_xbc, W), "m2_bc": cols(o_xbc + W, W),
        "m2_z": cols(o_z, W), "gla_qk": cols(o_q, W), "gla_v": cols(o_v, W),
        "gla_gate": cols(o_gg, W),
        "rw_wa": cols(o_rw + 3 * W, rank2), "m2_dt": _pad_cols(cols(o_dt, M2_HEADS), LANE),
        "gla_gk": _pad_cols(cols(o_gk, GLA_RANK), LANE),
    }
    w_logit = cols(o_merge, LOGIT_WIDTH).astype(BF16)
    w_mixin = jnp.concatenate([segs[n] for n in SEG512] + [segs[n] for n in SEG128], axis=1)
    w_mixin = _pad_cols(w_mixin, MIXIN_WIDTH).astype(BF16)

    row = lambda a: a.reshape(1, -1)
    mu = rw_mu[l]
    head_of_ch = jnp.arange(MXU_DIM) // RW_DIM
    ones_bd = (head_of_ch[:, None] == head_of_ch[None, :]).astype(BF16)
    expand = (jnp.arange(LANE)[:, None] == (jnp.arange(W) // M2_DIM)[None, :]).astype(BF16)
    cw = m2_conv_w[l]
    cb = m2_conv_b[l]
    return {
        "w_logit": w_logit, "w_mixin": w_mixin,
        "mu_r": row(mu[:W]), "mu_k": row(mu[W:2 * W]), "mu_v": row(mu[2 * W:3 * W]),
        "mu_wa": row(mu[3 * W:]),
        "w0": row(rw_w0[l]), "w_up": _pad_rows_at(rw_w_up[l], 0, LANE),
        "a0": row(rw_a0[l]), "a_up": _pad_rows_at(rw_a_up[l], RW_RANK, LANE),
        "k_k": row(rw_k_k[l]), "k_a": row(rw_k_a[l]), "r_k": row(rw_r_k[l]),
        "ln_g": row(rw_ln_g[l]), "ln_b": row(rw_ln_b[l]), "ones_bd": ones_bd,
        "cw_x": cw[:, :W], "cw_bc": cw[:, W:], "cb_x": row(cb[:W]), "cb_bc": row(cb[W:]),
        "dt_bias": _pad_cols(row(m2_dt_bias[l]), LANE), "a_log": _pad_cols(row(m2_a_log[l]), LANE),
        "d_skip": row(jnp.repeat(m2_d_skip[l], M2_DIM)), "m2_norm_w": row(m2_norm_w[l]),
        "expand": expand,
        "gk_up": _pad_rows_at(gla_gk_up[l], 0, LANE), "gk_b": row(gla_gk_b[l]),
        "gla_norm_w": row(jnp.tile(gla_norm_w[l], GLA_HEADS)),
        "w_branch": w_branch[l].astype(BF16), "w_out": w_out[l].astype(BF16),
        "post_g": row(post_g[l]), "post_b": row(post_b[l]),
    }


def kernel(x, c, ada_w, ada_b, w_in, rw_mu, rw_w0, rw_w_up, rw_a0, rw_a_up, rw_k_k, rw_k_a, rw_r_k, rw_ln_g, rw_ln_b, m2_conv_w, m2_conv_b, m2_dt_bias, m2_a_log, m2_d_skip, m2_norm_w, gla_gk_up, gla_gk_b, gla_norm_w, w_branch, w_out, post_g, post_b):
    bsz, seq, d = x.shape
    depth = ada_w.shape[0]
    alpha = (2.0 * depth) ** 0.25
    mod_all = _adaln(c, ada_w, ada_b).reshape(depth, bsz, 3, d)
    x2 = x.reshape(bsz * seq, d)
    for l in range(depth):
        p = _layer_params(l, w_in, rw_mu, rw_w0, rw_w_up, rw_a0, rw_a_up, rw_k_k, rw_k_a,
                          rw_r_k, rw_ln_g, rw_ln_b, m2_conv_w, m2_conv_b, m2_dt_bias, m2_a_log,
                          m2_d_skip, m2_norm_w, gla_gk_up, gla_gk_b, gla_norm_w, w_branch, w_out,
                          post_g, post_b)
        mod = mod_all[l]
        logits, mixin = _inproj(x2, mod, p["w_logit"], p["w_mixin"], seq)
        y_rw, y_m2, y_gla = _mixers(mixin, p, bsz, seq)
        x2 = _merge(y_rw, y_m2, y_gla, logits, x2, mod, p, seq, alpha)
    return x2.reshape(bsz, seq, d)
```
